```python
import jax, jax.numpy as jnp
from jax import lax
import numpy as np


D_MODEL = 1024
BATCH = 16
SEQ = 2048
DEPTH = 1

CONV_CH = D_MODEL
CONV_WIDTH = 31
N_HEADS = 16
HEAD_DIM = 64
ATTN_DIM = N_HEADS * HEAD_DIM
DILATION_GROUPS = ((128, 1), (512, 4), (2048, 16))
ATTN_BLOCK = 128
D_FF = 2816
FFN_CONV_WIDTH = 3
EPS = 1e-6
SPLITS = (2 * CONV_CH, 2 * CONV_CH + ATTN_DIM, 2 * CONV_CH + 2 * ATTN_DIM, 2 * CONV_CH + 3 * ATTN_DIM)
IN_COLS = 2 * CONV_CH + 3 * ATTN_DIM + 2 * D_MODEL

kernel_name = "hybrid_conformer_conv_dilated_attn_block"


def rms_norm(x, g):
    xf = x.astype(jnp.float32)
    xf = xf * lax.rsqrt(jnp.mean(xf * xf, axis=-1, keepdims=True) + EPS)
    return (xf * g.astype(jnp.float32)).astype(x.dtype)


def causal_depthwise_conv(u, w, b):
    width, ch = w.shape
    out = lax.conv_general_dilated(
        u, w[:, None, :].astype(u.dtype), window_strides=(1,), padding=[(width - 1, 0)],
        dimension_numbers=('NWC', 'WIO', 'NWC'), feature_group_count=ch)
    return out + b.astype(u.dtype)


def alibi_slopes(n_heads):
    return 2.0 ** (-8.0 * jnp.arange(1, n_heads + 1, dtype=jnp.float32) / n_heads)


def dilated_window_attention(q, k, v, slopes, window, dilation):
    B, S, H, Dh = q.shape
    n_back = window // dilation
    L = S // dilation
    nb = -(-L // ATTN_BLOCK)
    Lp = nb * ATTN_BLOCK

    def to_sub(t):
        t = t.reshape(B, L, dilation, H, Dh).transpose(0, 2, 1, 3, 4).reshape(B * dilation, L, H, Dh)
        t = jnp.pad(t, ((0, 0), (0, Lp - L), (0, 0), (0, 0)))
        return t.reshape(B * dilation, nb, ATTN_BLOCK, H, Dh)

    def with_prev(t):
        prev = jnp.pad(t, ((0, 0), (1, 0), (0, 0), (0, 0), (0, 0)))[:, :-1]
        return jnp.concatenate([prev, t], axis=2)

    qb = to_sub(q)
    kc = with_prev(to_sub(k))
    vc = with_prev(to_sub(v))

    scores = jnp.einsum('nbqhd,nbkhd->nbhqk', qb, kc)
    steps = (jnp.arange(ATTN_BLOCK)[:, None] + ATTN_BLOCK) - jnp.arange(2 * ATTN_BLOCK)[None, :]
    valid = (steps >= 0) & (steps <= n_back)
    first_block = (jnp.arange(nb) == 0)[:, None, None]
    prev_cols = (jnp.arange(2 * ATTN_BLOCK) < ATTN_BLOCK)[None, None, :]
    valid = valid[None] & ~(first_block & prev_cols)
    dist = (steps * dilation).astype(jnp.float32)
    scores = scores - slopes[:, None, None] * dist
    scores = jnp.where(valid[None, :, None], scores, -jnp.inf)
    m = jnp.max(scores, axis=-1, keepdims=True)
    p = jnp.exp(scores - m)
    den = jnp.sum(p, axis=-1)
    o = jnp.einsum('nbhqk,nbkhd->nbqhd', p, vc) / jnp.swapaxes(den, 2, 3)[..., None]
    lse = jnp.swapaxes(m[..., 0] + jnp.log(den), 2, 3)

    def from_sub(t):
        rest = t.shape[3:]
        t = t.reshape(B, dilation, Lp, *rest)[:, :, :L]
        return jnp.swapaxes(t, 1, 2).reshape(B, S, *rest)

    return from_sub(o), from_sub(lse)


def setup_inputs(seed: int = 0) -> dict:
    key = jax.random.key(seed)
    ks = jax.random.split(key, 18)
    f32 = jnp.float32

    def nrm(k, shape, scale):
        return jax.random.normal(k, shape, f32) * scale

    return {
        'x': nrm(ks[0], (BATCH, SEQ, D_MODEL), 1.0),
        'norm1_g': 1.0 + nrm(ks[1], (DEPTH, D_MODEL), 0.1),
        'w_in': nrm(ks[2], (DEPTH, D_MODEL, IN_COLS), D_MODEL ** -0.5),
        'gate_b': nrm(ks[3], (DEPTH, 2 * D_MODEL), 0.02),
        'conv_w': nrm(ks[4], (DEPTH, CONV_WIDTH, CONV_CH), CONV_WIDTH ** -0.5),
        'conv_b': nrm(ks[5], (DEPTH, CONV_CH), 0.02),
        'conv_norm_g': 1.0 + nrm(ks[6], (DEPTH, CONV_CH), 0.1),
        'w_conv_out': nrm(ks[7], (DEPTH, CONV_CH, D_MODEL), CONV_CH ** -0.5),
        'q_norm_g': 1.0 + nrm(ks[8], (DEPTH, HEAD_DIM), 0.1),
        'k_norm_g': 1.0 + nrm(ks[9], (DEPTH, HEAD_DIM), 0.1),
        'w_attn_out': nrm(ks[10], (DEPTH, ATTN_DIM, D_MODEL), ATTN_DIM ** -0.5),
        'w_out': nrm(ks[11], (DEPTH, D_MODEL, D_MODEL), D_MODEL ** -0.5),
        'norm2_g': 1.0 + nrm(ks[12], (DEPTH, D_MODEL), 0.1),
        'w_up': nrm(ks[13], (DEPTH, D_MODEL, 2 * D_FF), D_MODEL ** -0.5),
        'ffn_conv_w': nrm(ks[14], (DEPTH, FFN_CONV_WIDTH, 2 * D_FF), FFN_CONV_WIDTH ** -0.5),
        'ffn_conv_b': nrm(ks[15], (DEPTH, 2 * D_FF), 0.02),
        'w_down': nrm(ks[16], (DEPTH, D_FF, D_MODEL), D_FF ** -0.5),
    }


def reference(x, norm1_g, w_in, gate_b, conv_w, conv_b, conv_norm_g, w_conv_out,
              q_norm_g, k_norm_g, w_attn_out, w_out, norm2_g, w_up, ffn_conv_w,
              ffn_conv_b, w_down):
    B, S, _ = x.shape
    slopes = alibi_slopes(N_HEADS)
    for l in range(DEPTH):
        h = rms_norm(x, norm1_g[l])
        z = h @ w_in[l].astype(h.dtype)
        glu_in, q, k, v, gate_logits = jnp.split(z, SPLITS, axis=-1)

        a_val, a_gate = jnp.split(glu_in, 2, axis=-1)
        a = a_val * jax.nn.sigmoid(a_gate)
        a = causal_depthwise_conv(a, conv_w[l], conv_b[l])
        a = jax.nn.silu(rms_norm(a, conv_norm_g[l]))
        y_a = a @ w_conv_out[l].astype(a.dtype)

        q = rms_norm(q.reshape(B, S, N_HEADS, HEAD_DIM), q_norm_g[l]).astype(jnp.float32) * HEAD_DIM ** -0.5
        k = rms_norm(k.reshape(B, S, N_HEADS, HEAD_DIM), k_norm_g[l]).astype(jnp.float32)
        v = v.reshape(B, S, N_HEADS, HEAD_DIM).astype(jnp.float32)
        outs, lses = [], []
        for window, dilation in DILATION_GROUPS:
            o_g, lse_g = dilated_window_attention(q, k, v, slopes, window, dilation)
            outs.append(o_g)
            lses.append(lse_g)
        mix = jax.nn.softmax(jnp.stack(lses), axis=0)
        o = jnp.einsum('gbsh,gbshd->bshd', mix, jnp.stack(outs))
        o = o.astype(x.dtype).reshape(B, S, ATTN_DIM)
        y_b = o @ w_attn_out[l].astype(o.dtype)

        g = jax.nn.sigmoid(gate_logits + gate_b[l].astype(gate_logits.dtype))
        g_a, g_b = jnp.split(g, 2, axis=-1)
        x = x + (g_a * y_a + g_b * y_b) @ w_out[l].astype(x.dtype)

        h = rms_norm(x, norm2_g[l])
        u = causal_depthwise_conv(h @ w_up[l].astype(h.dtype), ffn_conv_w[l], ffn_conv_b[l])
        u_val, u_gate = jnp.split(u, 2, axis=-1)
        x = x + (jax.nn.silu(u_gate) * u_val) @ w_down[l].astype(x.dtype)
    return x
```

```python
import functools

import jax
import jax.numpy as jnp
from jax import lax
from jax.experimental import pallas as pl
from jax.experimental.pallas import tpu as pltpu

F32 = jnp.float32
BF16 = jnp.bfloat16

D_MODEL = 1024
CONV_CH = D_MODEL
CONV_WIDTH = 31
N_HEADS = 16
HEAD_DIM = 64
ATTN_DIM = N_HEADS * HEAD_DIM
DILATION_GROUPS = ((128, 1), (512, 4), (2048, 16))
ATTN_BLOCK = 128
D_FF = 2816
FFN_CONV_WIDTH = 3
EPS = 1e-6

LANES = 128
SUBLANES = 8
VMEM_LIMIT_BYTES = 56 * 1024 * 1024
MASK_STEPS = 1e30

TM_IN = 512
TM_OUT = 256
TM_FFN = 256
HALO = 32
NORM_CHUNK = 256
FFN_CHUNK = 256


def _dot(a, b):
    return jnp.dot(a, b, preferred_element_type=F32)


def _sigmoid(x):
    return 1.0 / (1.0 + jnp.exp(-x))


def _resident(shape):
    return pl.BlockSpec(shape, lambda *_: (0,) * len(shape), pipeline_mode=pl.Buffered(1))


def _in_proj_kernel(x_ref, g1_ref, w_ref, gb_ref, qg_ref, kg_ref, pm_ref,
                    a_ref, q_ref, k_ref, v_ref, g_ref):
    x = x_ref[...]
    ms = jnp.mean(x * x, axis=-1, keepdims=True)
    h = (x * lax.rsqrt(ms + EPS) * g1_ref[...]).astype(BF16)

    c2 = 2 * CONV_CH
    for c in range(0, CONV_CH, 512):
        val = _dot(h, w_ref[:, c:c + 512])
        gate = _dot(h, w_ref[:, CONV_CH + c:CONV_CH + c + 512])
        a_ref[:, c:c + 512] = (val * _sigmoid(gate)).astype(BF16)

    def head_norm(col0, gain_ref, scale, out_ref):
        for c in range(0, ATTN_DIM, NORM_CHUNK):
            z = _dot(h, w_ref[:, col0 + c:col0 + c + NORM_CHUNK])
            msq = _dot((z * z).astype(BF16), pm_ref[...])
            zn = z * lax.rsqrt(msq + EPS) * gain_ref[:, c:c + NORM_CHUNK]
            out_ref[:, c:c + NORM_CHUNK] = (zn * scale).astype(BF16)

    head_norm(c2, qg_ref, HEAD_DIM ** -0.5, q_ref)
    head_norm(c2 + ATTN_DIM, kg_ref, 1.0, k_ref)

    for c in range(0, ATTN_DIM, 512):
        col = c2 + 2 * ATTN_DIM + c
        v_ref[:, c:c + 512] = _dot(h, w_ref[:, col:col + 512]).astype(BF16)

    for c in range(0, 2 * D_MODEL, 512):
        col = c2 + 3 * ATTN_DIM + c
        z = _dot(h, w_ref[:, col:col + 512]) + gb_ref[:, c:c + 512]
        g_ref[:, c:c + 512] = _sigmoid(z).astype(BF16)


def _in_proj(x2d, norm1_g, w_in, gate_b, q_norm_g, k_norm_g):
    t = x2d.shape[0]
    in_cols = w_in.shape[1]
    idx = jnp.arange(NORM_CHUNK) // HEAD_DIM
    pmat = jnp.where(idx[:, None] == idx[None, :], 1.0 / HEAD_DIM, 0.0).astype(BF16)
    qg = jnp.tile(q_norm_g.astype(F32), N_HEADS).reshape(1, ATTN_DIM)
    kg = jnp.tile(k_norm_g.astype(F32), N_HEADS).reshape(1, ATTN_DIM)
    row = lambda i: (i, 0)
    tok = lambda width, dt: jax.ShapeDtypeStruct((t, width), dt)
    return pl.pallas_call(
        _in_proj_kernel,
        grid=(t // TM_IN,),
        in_specs=[
            pl.BlockSpec((TM_IN, D_MODEL), row),
            _resident((1, D_MODEL)),
            _resident((D_MODEL, in_cols)),
            _resident((1, 2 * D_MODEL)),
            _resident((1, ATTN_DIM)),
            _resident((1, ATTN_DIM)),
            _resident((NORM_CHUNK, NORM_CHUNK)),
        ],
        out_specs=[
            pl.BlockSpec((TM_IN, CONV_CH), row),
            pl.BlockSpec((TM_IN, ATTN_DIM), row),
            pl.BlockSpec((TM_IN, ATTN_DIM), row),
            pl.BlockSpec((TM_IN, ATTN_DIM), row),
            pl.BlockSpec((TM_IN, 2 * D_MODEL), row),
        ],
        out_shape=[tok(CONV_CH, BF16), tok(ATTN_DIM, BF16), tok(ATTN_DIM, BF16),
                   tok(ATTN_DIM, BF16), tok(2 * D_MODEL, BF16)],
        compiler_params=pltpu.CompilerParams(
            dimension_semantics=("parallel",), vmem_limit_bytes=VMEM_LIMIT_BYTES),
        name="in_proj",
    )(x2d, norm1_g.reshape(1, D_MODEL), w_in.astype(BF16), gate_b.reshape(1, 2 * D_MODEL), qg, kg, pmat)


def _attn_kernel(slopes_ref, q_ref, k_ref, v_ref, o_ref, qf, kf, vf, og, lg, *, seq):
    blk = ATTN_BLOCK
    pad = max(d for _, d in DILATION_GROUPS if seq // d > blk) * blk
    hp = pl.program_id(1)

    qf[...] = q_ref[...].astype(F32)
    kf[0:pad, :] = jnp.zeros((pad, LANES), F32)
    vf[0:pad, :] = jnp.zeros((pad, LANES), F32)
    kf[pad:pad + seq, :] = k_ref[...].astype(F32)
    vf[pad:pad + seq, :] = v_ref[...].astype(F32)

    lane_q = lax.broadcasted_iota(jnp.int32, (blk, LANES), 1)
    first_q = lane_q < HEAD_DIM
    rows = lax.broadcasted_iota(jnp.int32, (blk, 2 * blk), 0)
    cols = lax.broadcasted_iota(jnp.int32, (blk, 2 * blk), 1)
    steps2 = rows + blk - cols
    rows1 = lax.broadcasted_iota(jnp.int32, (blk, blk), 0)
    cols1 = lax.broadcasted_iota(jnp.int32, (blk, blk), 1)
    steps1 = rows1 - cols1

    def one_block(qt, kt, vt, dist, c_pair):
        lane_k = lax.broadcasted_iota(jnp.int32, kt.shape, 1)
        first_k = lane_k < HEAD_DIM
        res, mx = [], []
        for hh in range(2):
            sel_q = first_q if hh == 0 else ~first_q
            sel_k = first_k if hh == 0 else ~first_k
            qh = jnp.where(sel_q, qt, 0.0).astype(BF16)
            s = lax.dot_general(qh, kt, (((1,), (1,)), ((), ())), preferred_element_type=F32)
            s = s - c_pair[hh] * dist
            m = jnp.max(s, axis=-1, keepdims=True)
            p = jnp.exp(s - m).astype(BF16)
            vh = jnp.where(sel_k, vt, jnp.ones_like(vt))
            res.append(_dot(p, vh))
            mx.append(m)
        acc = jnp.where(first_q, res[0], res[1])
        den = pltpu.roll(jnp.where(first_q, res[1], res[0]), HEAD_DIM, 1)
        mm = jnp.where(first_q, mx[0], mx[1])
        return acc / den, mm + jnp.log(den)

    for g, (window, dil) in enumerate(DILATION_GROUPS):
        n_back = window // dil
        sub_len = seq // dil
        nb = sub_len // blk
        c_pair = [slopes_ref[2 * hp + hh] * float(dil) for hh in range(2)]
        if nb == 1:
            dist1 = jnp.where((steps1 >= 0) & (steps1 <= n_back), steps1.astype(F32), MASK_STEPS)

            def body1(r, carry, dil=dil, g=g, dist1=dist1, c_pair=c_pair):
                qt = qf[pl.ds(r, blk, stride=dil), :]
                kt = kf[pl.ds(pad + r, blk, stride=dil), :].astype(BF16)
                vt = vf[pl.ds(pad + r, blk, stride=dil), :].astype(BF16)
                o, l = one_block(qt, kt, vt, dist1, c_pair)
                og[g, pl.ds(r, blk, stride=dil), :] = o
                lg[g, pl.ds(r, blk, stride=dil), :] = l
                return carry

            lax.fori_loop(0, dil, body1, 0)
        else:
            valid = (steps2 >= 0) & (steps2 <= n_back)
            dist2 = jnp.where(valid, steps2.astype(F32), MASK_STEPS)
            dist2_first = jnp.where(valid & (cols >= blk), steps2.astype(F32), MASK_STEPS)

            def body2(it, carry, dil=dil, g=g, nb=nb, dist2=dist2, dist2_first=dist2_first, c_pair=c_pair):
                r = it // nb
                jb = it % nb
                q0 = r + jb * (blk * dil)
                k0 = pad + q0 - blk * dil
                if dil == 1:
                    qt = qf[pl.ds(q0, blk), :]
                    kt = kf[pl.ds(k0, 2 * blk), :].astype(BF16)
                    vt = vf[pl.ds(k0, 2 * blk), :].astype(BF16)
                else:
                    qt = qf[pl.ds(q0, blk, stride=dil), :]
                    kt = kf[pl.ds(k0, 2 * blk, stride=dil), :].astype(BF16)
                    vt = vf[pl.ds(k0, 2 * blk, stride=dil), :].astype(BF16)
                dist = jnp.where(jb == 0, dist2_first, dist2)
                o, l = one_block(qt, kt, vt, dist, c_pair)
                if dil == 1:
                    og[g, pl.ds(q0, blk), :] = o
                    lg[g, pl.ds(q0, blk), :] = l
                else:
                    og[g, pl.ds(q0, blk, stride=dil), :] = o
                    lg[g, pl.ds(q0, blk, stride=dil), :] = l
                return carry

            lax.fori_loop(0, dil * nb, body2, 0)

    ngroups = len(DILATION_GROUPS)

    def mix_body(i, carry):
        r0 = pl.multiple_of(i * blk, blk)
        ls = [lg[g, pl.ds(r0, blk), :] for g in range(ngroups)]
        mmax = functools.reduce(jnp.maximum, ls)
        ws = [jnp.exp(l - mmax) for l in ls]
        wsum = functools.reduce(lambda a, b: a + b, ws)
        acc = ws[0] * og[0, pl.ds(r0, blk), :]
        for g in range(1, ngroups):
            acc = acc + ws[g] * og[g, pl.ds(r0, blk), :]
        o_ref[pl.ds(r0, blk), :] = (acc / wsum).astype(o_ref.dtype)
        return carry

    lax.fori_loop(0, seq // blk, mix_body, 0)


def _attention(q, k, v, batch, seq):
    slopes = 2.0 ** (-8.0 * jnp.arange(1, N_HEADS + 1, dtype=F32) / N_HEADS)
    q3, k3, v3 = (t.reshape(batch, seq, ATTN_DIM) for t in (q, k, v))
    pad = max(d for _, d in DILATION_GROUPS if seq // d > ATTN_BLOCK) * ATTN_BLOCK
    ngroups = len(DILATION_GROUPS)
    blk = pl.BlockSpec((None, seq, LANES), lambda b, h: (b, 0, h))
    out = pl.pallas_call(
        functools.partial(_attn_kernel, seq=seq),
        grid=(batch, ATTN_DIM // LANES),
        in_specs=[pl.BlockSpec(memory_space=pltpu.SMEM), blk, blk, blk],
        out_specs=blk,
        out_shape=jax.ShapeDtypeStruct((batch, seq, ATTN_DIM), BF16),
        scratch_shapes=[
            pltpu.VMEM((seq, LANES), F32),
            pltpu.VMEM((pad + seq, LANES), F32),
            pltpu.VMEM((pad + seq, LANES), F32),
            pltpu.VMEM((ngroups, seq, LANES), F32),
            pltpu.VMEM((ngroups, seq, LANES), F32),
        ],
        compiler_params=pltpu.CompilerParams(
            dimension_semantics=("parallel", "parallel"), vmem_limit_bytes=VMEM_LIMIT_BYTES),
        name="attention",
    )(slopes, q3, k3, v3)
    return out.reshape(batch * seq, ATTN_DIM)


def _token_out_kernel(x_ref, a_ref, halo_ref, o_ref, g_ref, cw_ref, cb_ref, cg_ref,
                      wc_ref, wa_ref, wo_ref, g2_ref, x1_ref, h2_ref, abuf, cbuf, *, tiles_per_seq):
    tm = TM_OUT
    first = (pl.program_id(0) % tiles_per_seq) == 0
    halo = halo_ref[...].astype(F32)
    abuf[0:HALO, :] = jnp.where(first, 0.0, halo)
    abuf[HALO:HALO + tm, :] = a_ref[...].astype(F32)

    off = HALO - (CONV_WIDTH - 1)

    def conv_cols(c, carry):
        c0 = pl.multiple_of(c * LANES, LANES)
        acc = jnp.zeros((tm, LANES), F32) + cb_ref[:, pl.ds(c0, LANES)]
        for j in range(CONV_WIDTH):
            acc = acc + cw_ref[pl.ds(j, 1), pl.ds(c0, LANES)] * abuf[pl.ds(off + j, tm), pl.ds(c0, LANES)]
        cbuf[:, pl.ds(c0, LANES)] = acc
        return carry

    lax.fori_loop(0, CONV_CH // LANES, conv_cols, 0)

    cv = cbuf[...]
    ms = jnp.mean(cv * cv, axis=-1, keepdims=True)
    cn = cv * lax.rsqrt(ms + EPS) * cg_ref[...]
    ac = (cn * _sigmoid(cn)).astype(BF16)

    ya = _dot(ac, wc_ref[...])
    yb = _dot(o_ref[...], wa_ref[...])
    ga = g_ref[:, 0:D_MODEL].astype(F32)
    gb = g_ref[:, D_MODEL:2 * D_MODEL].astype(F32)
    mix = (ga * ya + gb * yb).astype(BF16)
    x1 = x_ref[...] + _dot(mix, wo_ref[...])
    x1_ref[...] = x1
    ms2 = jnp.mean(x1 * x1, axis=-1, keepdims=True)
    h2_ref[...] = (x1 * lax.rsqrt(ms2 + EPS) * g2_ref[...]).astype(BF16)


def _token_out(x2d, a, o, g, conv_w, conv_b, conv_norm_g, w_conv_out, w_attn_out, w_out, norm2_g, seq):
    t = x2d.shape[0]
    tm = TM_OUT
    tiles_per_seq = seq // tm
    halo_per_tile = tm // HALO
    cw = jnp.zeros((HALO, CONV_CH), F32).at[:CONV_WIDTH].set(conv_w.astype(F32))
    row = lambda i: (i, 0)
    return pl.pallas_call(
        functools.partial(_token_out_kernel, tiles_per_seq=tiles_per_seq),
        grid=(t // tm,),
        in_specs=[
            pl.BlockSpec((tm, D_MODEL), row),
            pl.BlockSpec((tm, CONV_CH), row),
            pl.BlockSpec((HALO, CONV_CH), lambda i: (jnp.maximum(i * halo_per_tile - 1, 0), 0)),
            pl.BlockSpec((tm, ATTN_DIM), row),
            pl.BlockSpec((tm, 2 * D_MODEL), row),
            _resident((HALO, CONV_CH)),
            _resident((1, CONV_CH)),
            _resident((1, CONV_CH)),
            _resident((CONV_CH, D_MODEL)),
            _resident((ATTN_DIM, D_MODEL)),
            _resident((D_MODEL, D_MODEL)),
            _resident((1, D_MODEL)),
        ],
        out_specs=[pl.BlockSpec((tm, D_MODEL), row), pl.BlockSpec((tm, D_MODEL), row)],
        out_shape=[jax.ShapeDtypeStruct((t, D_MODEL), F32), jax.ShapeDtypeStruct((t, D_MODEL), BF16)],
        scratch_shapes=[pltpu.VMEM((HALO + tm, CONV_CH), F32), pltpu.VMEM((tm, CONV_CH), F32)],
        compiler_params=pltpu.CompilerParams(
            dimension_semantics=("parallel",), vmem_limit_bytes=VMEM_LIMIT_BYTES),
        name="token_out",
    )(x2d, a, a, o, g, cw, conv_b.reshape(1, CONV_CH), conv_norm_g.reshape(1, CONV_CH),
      w_conv_out.astype(BF16), w_attn_out.astype(BF16), w_out.astype(BF16), norm2_g.reshape(1, D_MODEL))


def _ffn_kernel(x1_ref, h2_ref, wup_ref, fw_ref, fb_ref, wdn_ref, x2_ref, ubuf, carry, act, *, tiles_per_seq):
    tm = TM_FFN
    first = (pl.program_id(0) % tiles_per_seq) == 0

    @pl.when(first)
    def _():
        carry[...] = jnp.zeros_like(carry)

    h2 = h2_ref[...]
    top = SUBLANES
    taps = FFN_CONV_WIDTH

    def conv(col0):
        u = _dot(h2, wup_ref[:, col0:col0 + FFN_CHUNK])
        ubuf[0:top, :] = carry[:, col0:col0 + FFN_CHUNK]
        ubuf[top:top + tm, :] = u
        carry[:, col0:col0 + FFN_CHUNK] = u[tm - top:tm, :]
        out = fb_ref[:, col0:col0 + FFN_CHUNK] + fw_ref[taps - 1:taps, col0:col0 + FFN_CHUNK] * u
        for j in range(taps - 1):
            shift = taps - 1 - j
            out = out + fw_ref[j:j + 1, col0:col0 + FFN_CHUNK] * ubuf[top - shift:top - shift + tm, :]
        return out

    for c in range(0, D_FF, FFN_CHUNK):
        uv = conv(c)
        ug = conv(D_FF + c)
        act[:, c:c + FFN_CHUNK] = (ug * _sigmoid(ug) * uv).astype(BF16)

    x2_ref[...] = x1_ref[...] + _dot(act[...], wdn_ref[...])


def _ffn(x1, h2, w_up, ffn_conv_w, ffn_conv_b, w_down, seq):
    t = x1.shape[0]
    tm = TM_FFN
    fw = jnp.zeros((SUBLANES, 2 * D_FF), F32).at[:FFN_CONV_WIDTH].set(ffn_conv_w.astype(F32))
    row = lambda i: (i, 0)
    return pl.pallas_call(
        functools.partial(_ffn_kernel, tiles_per_seq=seq // tm),
        grid=(t // tm,),
        in_specs=[
            pl.BlockSpec((tm, D_MODEL), row),
            pl.BlockSpec((tm, D_MODEL), row),
            _resident((D_MODEL, 2 * D_FF)),
            _resident((SUBLANES, 2 * D_FF)),
            _resident((1, 2 * D_FF)),
            _resident((D_FF, D_MODEL)),
        ],
        out_specs=pl.BlockSpec((tm, D_MODEL), row),
        out_shape=jax.ShapeDtypeStruct((t, D_MODEL), F32),
        scratch_shapes=[
            pltpu.VMEM((SUBLANES + tm, FFN_CHUNK), F32),
            pltpu.VMEM((SUBLANES, 2 * D_FF), F32),
            pltpu.VMEM((tm, D_FF), BF16),
        ],
        compiler_params=pltpu.CompilerParams(
            dimension_semantics=("arbitrary",), vmem_limit_bytes=VMEM_LIMIT_BYTES),
        name="ffn",
    )(x1, h2, w_up.astype(BF16), fw, ffn_conv_b.reshape(1, 2 * D_FF), w_down.astype(BF16))


def kernel(x, norm1_g, w_in, gate_b, conv_w, conv_b, conv_norm_g, w_conv_out, q_norm_g, k_norm_g,
           w_attn_out, w_out, norm2_g, w_up, ffn_conv_w, ffn_conv_b, w_down):
    batch, seq, d = x.shape
    assert d == D_MODEL and seq % TM_IN == 0 and seq % TM_OUT == 0 and seq % TM_FFN == 0
    assert all(seq % (dil * ATTN_BLOCK) == 0 for _, dil in DILATION_GROUPS)
    depth = norm1_g.shape[0]
    x2d = x.reshape(batch * seq, d)
    for l in range(depth):
        a, q, k, v, g = _in_proj(x2d, norm1_g[l], w_in[l], gate_b[l], q_norm_g[l], k_norm_g[l])
        o = _attention(q, k, v, batch, seq)
        x1, h2 = _token_out(x2d, a, o, g, conv_w[l], conv_b[l], conv_norm_g[l], w_conv_out[l],
                            w_attn_out[l], w_out[l], norm2_g[l], seq)
        x2d = _ffn(x1, h2, w_up[l], ffn_conv_w[l], ffn_conv_b[l], w_down[l], seq)
    return x2d.reshape(batch, seq, d)
```

```python
import functools

import jax
import jax.numpy as jnp
from jax import lax
from jax.experimental import pallas as pl
from jax.experimental.pallas import tpu as pltpu

F32 = jnp.float32
BF16 = jnp.bfloat16

D_MODEL = 1024
CONV_CH = D_MODEL
CONV_WIDTH = 31
N_HEADS = 16
HEAD_DIM = 64
ATTN_DIM = N_HEADS * HEAD_DIM
DILATION_GROUPS = ((128, 1), (512, 4), (2048, 16))
ATTN_BLOCK = 128
D_FF = 2816
FFN_CONV_WIDTH = 3
EPS = 1e-6

LANES = 128
SUBLANES = 8
VMEM_LIMIT_BYTES = 56 * 1024 * 1024
MASK_STEPS = 1e30

TM_IN = 512
TM_OUT = 256
TM_FFN = 256
HALO = 32
NORM_CHUNK = 256
FFN_CHUNK = 256
ATTN_UNROLL = 8
ATTN_LOOKAHEAD = 2
SPLIT = 4


def _dot(a, b):
    return jnp.dot(a, b, preferred_element_type=F32)


def _sigmoid(x):
    return 1.0 / (1.0 + jnp.exp(-x))


def _resident(shape):
    return pl.BlockSpec(shape, lambda *_: (0,) * len(shape), pipeline_mode=pl.Buffered(1))


def _in_proj_kernel(x_ref, g1_ref, w_ref, gb_ref, qg_ref, kg_ref, pm_ref,
                    a_ref, q_ref, k_ref, v_ref, g_ref):
    x = x_ref[...]
    ms = jnp.mean(x * x, axis=-1, keepdims=True)
    h = (x * lax.rsqrt(ms + EPS) * g1_ref[...]).astype(BF16)

    c2 = 2 * CONV_CH
    for c in range(0, CONV_CH, 512):
        val = _dot(h, w_ref[:, c:c + 512])
        gate = _dot(h, w_ref[:, CONV_CH + c:CONV_CH + c + 512])
        a_ref[:, c:c + 512] = (val * _sigmoid(gate)).astype(BF16)

    def head_norm(col0, gain_ref, scale, out_ref):
        for c in range(0, ATTN_DIM, NORM_CHUNK):
            z = _dot(h, w_ref[:, col0 + c:col0 + c + NORM_CHUNK])
            msq = _dot((z * z).astype(BF16), pm_ref[...])
            zn = z * lax.rsqrt(msq + EPS) * gain_ref[:, c:c + NORM_CHUNK]
            out_ref[:, c:c + NORM_CHUNK] = (zn * scale).astype(BF16)

    head_norm(c2, qg_ref, HEAD_DIM ** -0.5, q_ref)
    head_norm(c2 + ATTN_DIM, kg_ref, 1.0, k_ref)

    for c in range(0, ATTN_DIM, 512):
        col = c2 + 2 * ATTN_DIM + c
        v_ref[:, c:c + 512] = _dot(h, w_ref[:, col:col + 512]).astype(BF16)

    for c in range(0, 2 * D_MODEL, 512):
        col = c2 + 3 * ATTN_DIM + c
        z = _dot(h, w_ref[:, col:col + 512]) + gb_ref[:, c:c + 512]
        g_ref[:, c:c + 512] = _sigmoid(z).astype(BF16)


def _in_proj(x2d, norm1_g, w_in, gate_b, q_norm_g, k_norm_g):
    t = x2d.shape[0]
    in_cols = w_in.shape[1]
    idx = jnp.arange(NORM_CHUNK) // HEAD_DIM
    pmat = jnp.where(idx[:, None] == idx[None, :], 1.0 / HEAD_DIM, 0.0).astype(BF16)
    qg = jnp.tile(q_norm_g.astype(F32), N_HEADS).reshape(1, ATTN_DIM)
    kg = jnp.tile(k_norm_g.astype(F32), N_HEADS).reshape(1, ATTN_DIM)
    row = lambda i: (i, 0)
    tok = lambda width, dt: jax.ShapeDtypeStruct((t, width), dt)
    return pl.pallas_call(
        _in_proj_kernel,
        grid=(t // TM_IN,),
        in_specs=[
            pl.BlockSpec((TM_IN, D_MODEL), row),
            _resident((1, D_MODEL)),
            _resident((D_MODEL, in_cols)),
            _resident((1, 2 * D_MODEL)),
            _resident((1, ATTN_DIM)),
            _resident((1, ATTN_DIM)),
            _resident((NORM_CHUNK, NORM_CHUNK)),
        ],
        out_specs=[
            pl.BlockSpec((TM_IN, CONV_CH), row),
            pl.BlockSpec((TM_IN, ATTN_DIM), row),
            pl.BlockSpec((TM_IN, ATTN_DIM), row),
            pl.BlockSpec((TM_IN, ATTN_DIM), row),
            pl.BlockSpec((TM_IN, 2 * D_MODEL), row),
        ],
        out_shape=[tok(CONV_CH, BF16), tok(ATTN_DIM, BF16), tok(ATTN_DIM, BF16),
                   tok(ATTN_DIM, BF16), tok(2 * D_MODEL, BF16)],
        compiler_params=pltpu.CompilerParams(
            dimension_semantics=("parallel",), vmem_limit_bytes=VMEM_LIMIT_BYTES),
        name="in_proj",
    )(x2d, norm1_g.reshape(1, D_MODEL), w_in.astype(BF16), gate_b.reshape(1, 2 * D_MODEL), qg, kg, pmat)


def _attn_kernel(slopes_ref, q_ref, k_ref, v_ref, o_ref, sta, stb, qb, kb, vb, og, lg, *, seq):
    blk = ATTN_BLOCK
    ngroups = len(DILATION_GROUPS)
    dils = [d for _, d in DILATION_GROUPS]
    assert dils[0] == 1 and all(dils[g] == SPLIT * dils[g - 1] for g in range(1, ngroups))
    hp = pl.program_id(1)

    def kv_slot_rows(g):
        sub_len = seq // dils[g]
        return sub_len + (blk if sub_len > blk else 0)

    for t_ref, dst, is_kv in ((q_ref, qb, False), (k_ref, kb, True), (v_ref, vb, True)):
        cur, nxt = sta, stb
        cur[...] = t_ref[...].astype(F32)
        for g in range(ngroups):
            sub_len = seq // dils[g]
            if g > 0:
                prev_len = seq // dils[g - 1]
                for j in range(dils[g - 1]):
                    for r in range(SPLIT):
                        dst0 = (j * SPLIT + r) * sub_len
                        nxt[dst0:dst0 + sub_len, :] = cur[pl.ds(j * prev_len + r, sub_len, stride=SPLIT), :]
                cur, nxt = nxt, cur
            slot = kv_slot_rows(g) if is_kv else sub_len
            front = slot - sub_len
            for j in range(dils[g]):
                if front:
                    dst[g, j * slot:j * slot + front, :] = jnp.zeros((front, LANES), BF16)
                dst[g, j * slot + front:(j + 1) * slot, :] = cur[j * sub_len:(j + 1) * sub_len, :].astype(BF16)

    lane_q = lax.broadcasted_iota(jnp.int32, (blk, LANES), 1)
    first_q = lane_q < HEAD_DIM
    rows = lax.broadcasted_iota(jnp.int32, (blk, 2 * blk), 0)
    cols = lax.broadcasted_iota(jnp.int32, (blk, 2 * blk), 1)
    steps2 = rows + blk - cols
    rows1 = lax.broadcasted_iota(jnp.int32, (blk, blk), 0)
    cols1 = lax.broadcasted_iota(jnp.int32, (blk, blk), 1)
    steps1 = rows1 - cols1

    def scores(qt, kt, dist, c_pair):
        out = []
        for hh in range(2):
            sel_q = first_q if hh == 0 else ~first_q
            qh = jnp.where(sel_q, qt, jnp.zeros_like(qt))
            s = lax.dot_general(qh, kt, (((1,), (1,)), ((), ())), preferred_element_type=F32)
            out.append(s - c_pair[hh] * dist)
        return out

    def weighted_values(ss, vt):
        first_k = lax.broadcasted_iota(jnp.int32, vt.shape, 1) < HEAD_DIM
        res, mx = [], []
        for hh in range(2):
            m = jnp.max(ss[hh], axis=-1, keepdims=True)
            p = jnp.exp(ss[hh] - m).astype(BF16)
            vh = jnp.where(first_k if hh == 0 else ~first_k, vt, jnp.ones_like(vt))
            res.append(_dot(p, vh))
            mx.append(m)
        acc = jnp.where(first_q, res[0], res[1])
        den = pltpu.roll(jnp.where(first_q, res[1], res[0]), HEAD_DIM, 1)
        mm = jnp.where(first_q, mx[0], mx[1])
        return acc / den, mm + jnp.log(den)

    for g, (window, dil) in enumerate(DILATION_GROUPS):
        n_back = window // dil
        sub_len = seq // dil
        nb = sub_len // blk
        slot = kv_slot_rows(g)
        c_pair = [slopes_ref[2 * hp + hh] * float(dil) for hh in range(2)]
        if nb == 1:
            dist_a = jnp.where((steps1 >= 0) & (steps1 <= n_back), steps1.astype(F32), MASK_STEPS)
            dist_b = dist_a
            nk = blk
        else:
            valid = (steps2 >= 0) & (steps2 <= n_back)
            dist_a = jnp.where(valid, steps2.astype(F32), MASK_STEPS)
            dist_b = jnp.where(valid & (cols >= blk), steps2.astype(F32), MASK_STEPS)
            nk = 2 * blk

        def body(i, carry, g=g, nb=nb, nk=nk, slot=slot, dist_a=dist_a, dist_b=dist_b, c_pair=c_pair):
            def offsets(u):
                it = i * ATTN_UNROLL + u
                j = it // nb
                jb = it % nb
                return pl.multiple_of(it * blk, blk), pl.multiple_of(j * slot + jb * blk, blk), jb

            pending = {}
            for step in range(ATTN_UNROLL + ATTN_LOOKAHEAD):
                if step < ATTN_UNROLL:
                    q0, k0, jb = offsets(step)
                    dist = dist_a if nb == 1 else jnp.where(jb == 0, dist_b, dist_a)
                    pending[step] = scores(qb[g, pl.ds(q0, blk), :], kb[g, pl.ds(k0, nk), :], dist, c_pair)
                u = step - ATTN_LOOKAHEAD
                if u >= 0:
                    q0, k0, _ = offsets(u)
                    o, l = weighted_values(pending.pop(u), vb[g, pl.ds(k0, nk), :])
                    og[g, pl.ds(q0, blk), :] = o
                    lg[g, pl.ds(q0, blk), :] = l
            return carry

        lax.fori_loop(0, dil * nb // ATTN_UNROLL, body, 0)

    for res in (og, lg):
        for g in range(ngroups - 1, 0, -1):
            for gg in range(g, 0, -1):
                sub_len = seq // dils[gg]
                prev_len = seq // dils[gg - 1]
                sta[...] = res[g]
                for j in range(dils[gg - 1]):
                    for r in range(SPLIT):
                        src0 = (j * SPLIT + r) * sub_len
                        res[g, pl.ds(j * prev_len + r, sub_len, stride=SPLIT), :] = sta[src0:src0 + sub_len, :]

    def mix_body(i, carry):
        r0 = pl.multiple_of(i * blk, blk)
        ls = [lg[g, pl.ds(r0, blk), :] for g in range(ngroups)]
        mmax = functools.reduce(jnp.maximum, ls)
        ws = [jnp.exp(l - mmax) for l in ls]
        wsum = functools.reduce(lambda a, b: a + b, ws)
        acc = ws[0] * og[0, pl.ds(r0, blk), :]
        for g in range(1, ngroups):
            acc = acc + ws[g] * og[g, pl.ds(r0, blk), :]
        o_ref[pl.ds(r0, blk), :] = (acc / wsum).astype(o_ref.dtype)
        return carry

    lax.fori_loop(0, seq // blk, mix_body, 0)


def _attention(q, k, v, batch, seq):
    slopes = 2.0 ** (-8.0 * jnp.arange(1, N_HEADS + 1, dtype=F32) / N_HEADS)
    q3, k3, v3 = (t.reshape(batch, seq, ATTN_DIM) for t in (q, k, v))
    ngroups = len(DILATION_GROUPS)
    kv_rows = max(seq + d * (ATTN_BLOCK if seq // d > ATTN_BLOCK else 0) for _, d in DILATION_GROUPS)
    blk = pl.BlockSpec((None, seq, LANES), lambda b, h: (b, 0, h))
    out = pl.pallas_call(
        functools.partial(_attn_kernel, seq=seq),
        grid=(batch, ATTN_DIM // LANES),
        in_specs=[pl.BlockSpec(memory_space=pltpu.SMEM), blk, blk, blk],
        out_specs=blk,
        out_shape=jax.ShapeDtypeStruct((batch, seq, ATTN_DIM), BF16),
        scratch_shapes=[
            pltpu.VMEM((seq, LANES), F32),
            pltpu.VMEM((seq, LANES), F32),
            pltpu.VMEM((ngroups, seq, LANES), BF16),
            pltpu.VMEM((ngroups, kv_rows, LANES), BF16),
            pltpu.VMEM((ngroups, kv_rows, LANES), BF16),
            pltpu.VMEM((ngroups, seq, LANES), F32),
            pltpu.VMEM((ngroups, seq, LANES), F32),
        ],
        compiler_params=pltpu.CompilerParams(
            dimension_semantics=("parallel", "parallel"), vmem_limit_bytes=VMEM_LIMIT_BYTES),
        name="attention",
    )(slopes, q3, k3, v3)
    return out.reshape(batch * seq, ATTN_DIM)


def _token_out_kernel(x_ref, a_ref, halo_ref, o_ref, g_ref, cw_ref, cb_ref, cg_ref,
                      wc_ref, wa_ref, wo_ref, g2_ref, x1_ref, h2_ref, abuf, cbuf, *, tiles_per_seq):
    tm = TM_OUT
    first = (pl.program_id(0) % tiles_per_seq) == 0
    halo = halo_ref[...].astype(F32)
    abuf[0:HALO, :] = jnp.where(first, 0.0, halo)
    abuf[HALO:HALO + tm, :] = a_ref[...].astype(F32)

    off = HALO - (CONV_WIDTH - 1)

    def conv_cols(c, carry):
        c0 = pl.multiple_of(c * LANES, LANES)
        acc = jnp.zeros((tm, LANES), F32) + cb_ref[:, pl.ds(c0, LANES)]
        for j in range(CONV_WIDTH):
            acc = acc + cw_ref[pl.ds(j, 1), pl.ds(c0, LANES)] * abuf[pl.ds(off + j, tm), pl.ds(c0, LANES)]
        cbuf[:, pl.ds(c0, LANES)] = acc
        return carry

    lax.fori_loop(0, CONV_CH // LANES, conv_cols, 0)

    cv = cbuf[...]
    ms = jnp.mean(cv * cv, axis=-1, keepdims=True)
    cn = cv * lax.rsqrt(ms + EPS) * cg_ref[...]
    ac = (cn * _sigmoid(cn)).astype(BF16)

    ya = _dot(ac, wc_ref[...])
    yb = _dot(o_ref[...], wa_ref[...])
    ga = g_ref[:, 0:D_MODEL].astype(F32)
    gb = g_ref[:, D_MODEL:2 * D_MODEL].astype(F32)
    mix = (ga * ya + gb * yb).astype(BF16)
    x1 = x_ref[...] + _dot(mix, wo_ref[...])
    x1_ref[...] = x1
    ms2 = jnp.mean(x1 * x1, axis=-1, keepdims=True)
    h2_ref[...] = (x1 * lax.rsqrt(ms2 + EPS) * g2_ref[...]).astype(BF16)


def _token_out(x2d, a, o, g, conv_w, conv_b, conv_norm_g, w_conv_out, w_attn_out, w_out, norm2_g, seq):
    t = x2d.shape[0]
    tm = TM_OUT
    tiles_per_seq = seq // tm
    halo_per_tile = tm // HALO
    cw = jnp.zeros((HALO, CONV_CH), F32).at[:CONV_WIDTH].set(conv_w.astype(F32))
    row = lambda i: (i, 0)
    return pl.pallas_call(
        functools.partial(_token_out_kernel, tiles_per_seq=tiles_per_seq),
        grid=(t // tm,),
        in_specs=[
            pl.BlockSpec((tm, D_MODEL), row),
            pl.BlockSpec((tm, CONV_CH), row),
            pl.BlockSpec((HALO, CONV_CH), lambda i: (jnp.maximum(i * halo_per_tile - 1, 0), 0)),
            pl.BlockSpec((tm, ATTN_DIM), row),
            pl.BlockSpec((tm, 2 * D_MODEL), row),
            _resident((HALO, CONV_CH)),
            _resident((1, CONV_CH)),
            _resident((1, CONV_CH)),
            _resident((CONV_CH, D_MODEL)),
            _resident((ATTN_DIM, D_MODEL)),
            _resident((D_MODEL, D_MODEL)),
            _resident((1, D_MODEL)),
        ],
        out_specs=[pl.BlockSpec((tm, D_MODEL), row), pl.BlockSpec((tm, D_MODEL), row)],
        out_shape=[jax.ShapeDtypeStruct((t, D_MODEL), F32), jax.ShapeDtypeStruct((t, D_MODEL), BF16)],
        scratch_shapes=[pltpu.VMEM((HALO + tm, CONV_CH), F32), pltpu.VMEM((tm, CONV_CH), F32)],
        compiler_params=pltpu.CompilerParams(
            dimension_semantics=("parallel",), vmem_limit_bytes=VMEM_LIMIT_BYTES),
        name="token_out",
    )(x2d, a, a, o, g, cw, conv_b.reshape(1, CONV_CH), conv_norm_g.reshape(1, CONV_CH),
      w_conv_out.astype(BF16), w_attn_out.astype(BF16), w_out.astype(BF16), norm2_g.reshape(1, D_MODEL))


def _ffn_kernel(x1_ref, h2_ref, wup_ref, fw_ref, fb_ref, wdn_ref, x2_ref, ubuf, carry, act, *, tiles_per_seq):
    tm = TM_FFN
    first = (pl.program_id(0) % tiles_per_seq) == 0

    @pl.when(first)
    def _():
        carry[...] = jnp.zeros_like(carry)

    h2 = h2_ref[...]
    top = SUBLANES
    taps = FFN_CONV_WIDTH

    def conv(col0):
        u = _dot(h2, wup_ref[:, col0:col0 + FFN_CHUNK])
        ubuf[0:top, :] = carry[:, col0:col0 + FFN_CHUNK]
        ubuf[top:top + tm, :] = u
        carry[:, col0:col0 + FFN_CHUNK] = u[tm - top:tm, :]
        out = fb_ref[:, col0:col0 + FFN_CHUNK] + fw_ref[taps - 1:taps, col0:col0 + FFN_CHUNK] * u
        for j in range(taps - 1):
            shift = taps - 1 - j
            out = out + fw_ref[j:j + 1, col0:col0 + FFN_CHUNK] * ubuf[top - shift:top - shift + tm, :]
        return out

    for c in range(0, D_FF, FFN_CHUNK):
        uv = conv(c)
        ug = conv(D_FF + c)
        act[:, c:c + FFN_CHUNK] = (ug * _sigmoid(ug) * uv).astype(BF16)

    x2_ref[...] = x1_ref[...] + _dot(act[...], wdn_ref[...])


def _ffn(x1, h2, w_up, ffn_conv_w, ffn_conv_b, w_down, seq):
    t = x1.shape[0]
    tm = TM_FFN
    fw = jnp.zeros((SUBLANES, 2 * D_FF), F32).at[:FFN_CONV_WIDTH].set(ffn_conv_w.astype(F32))
    row = lambda i: (i, 0)
    return pl.pallas_call(
        functools.partial(_ffn_kernel, tiles_per_seq=seq // tm),
        grid=(t // tm,),
        in_specs=[
            pl.BlockSpec((tm, D_MODEL), row),
            pl.BlockSpec((tm, D_MODEL), row),
            _resident((D_MODEL, 2 * D_FF)),
            _resident((SUBLANES, 2 * D_FF)),
            _resident((1, 2 * D_FF)),
            _resident((D_FF, D_MODEL)),
        ],
        out_specs=pl.BlockSpec((tm, D_MODEL), row),
        out_shape=jax.ShapeDtypeStruct((t, D_MODEL), F32),
        scratch_shapes=[
            pltpu.VMEM((SUBLANES + tm, FFN_CHUNK), F32),
            pltpu.VMEM((SUBLANES, 2 * D_FF), F32),
            pltpu.VMEM((tm, D_FF), BF16),
        ],
        compiler_params=pltpu.CompilerParams(
            dimension_semantics=("arbitrary",), vmem_limit_bytes=VMEM_LIMIT_BYTES),
        name="ffn",
    )(x1, h2, w_up.astype(BF16), fw, ffn_conv_b.reshape(1, 2 * D_FF), w_down.astype(BF16))


def kernel(x, norm1_g, w_in, gate_b, conv_w, conv_b, conv_norm_g, w_conv_out, q_norm_g, k_norm_g,
           w_attn_out, w_out, norm2_g, w_up, ffn_conv_w, ffn_conv_b, w_down):
    batch, seq, d = x.shape
    assert d == D_MODEL and seq % TM_IN == 0 and seq % TM_OUT == 0 and seq % TM_FFN == 0
    assert all(seq % (dil * ATTN_BLOCK) == 0 for _, dil in DILATION_GROUPS)
    depth = norm1_g.shape[0]
    x2d = x.reshape(batch * seq, d)
    for l in range(depth):
        a, q, k, v, g = _in_proj(x2d, norm1_g[l], w_in[l], gate_b[l], q_norm_g[l], k_norm_g[l])
        o = _attention(q, k, v, batch, seq)
        x1, h2 = _token_out(x2d, a, o, g, conv_w[l], conv_b[l], conv_norm_g[l], w_conv_out[l],
                            w_attn_out[l], w_out[l], norm2_g[l], seq)
        x2d = _ffn(x1, h2, w_up[l], ffn_conv_w[l], ffn_conv_b[l], w_down[l], seq)
    return x2d.reshape(batch, seq, d)
```

```python
import functools

import jax
import jax.numpy as jnp
from jax import lax
from jax.experimental import pallas as pl
from jax.experimental.pallas import tpu as pltpu

F32 = jnp.float32
BF16 = jnp.bfloat16

D_MODEL = 1024
CONV_CH = D_MODEL
CONV_WIDTH = 31
N_HEADS = 16
HEAD_DIM = 64
ATTN_DIM = N_HEADS * HEAD_DIM
DILATION_GROUPS = ((128, 1), (512, 4), (2048, 16))
ATTN_BLOCK = 128
D_FF = 2816
FFN_CONV_WIDTH = 3
EPS = 1e-6

LANES = 128
SUBLANES = 8
VMEM_LIMIT_BYTES = 56 * 1024 * 1024
MASK_STEPS = 1e30
LOG2E = 1.4426950408889634
LN2 = 0.6931471805599453

TM_IN = 512
TM_OUT = 256
TM_FFN = 256
HALO = 32
NORM_CHUNK = 256
FFN_CHUNK = 256
ATTN_LOOKAHEAD = 2
SPLIT = 4


def _dot(a, b):
    return jnp.dot(a, b, preferred_element_type=F32)


def _sigmoid(x):
    return 1.0 / (1.0 + jnp.exp(-x))


def _resident(shape):
    return pl.BlockSpec(shape, lambda *_: (0,) * len(shape), pipeline_mode=pl.Buffered(1))


def _in_proj_kernel(x_ref, g1_ref, w_ref, gb_ref, qg_ref, kg_ref, pm_ref,
                    a_ref, q_ref, k_ref, v_ref, g_ref):
    x = x_ref[...]
    ms = jnp.mean(x * x, axis=-1, keepdims=True)
    h = (x * lax.rsqrt(ms + EPS) * g1_ref[...]).astype(BF16)

    tm = x.shape[0]
    c2 = 2 * CONV_CH
    n_chunks = ATTN_DIM // NORM_CHUNK

    def head_msq(z):
        sq = (z * z).astype(BF16)
        stacked = jnp.concatenate([sq[:, c * NORM_CHUNK:(c + 1) * NORM_CHUNK] for c in range(n_chunks)], axis=0)
        msq = _dot(stacked, pm_ref[...])
        return jnp.concatenate([msq[c * tm:(c + 1) * tm, :] for c in range(n_chunks)], axis=1)

    zglu = _dot(h, w_ref[:, 0:c2])
    zq = _dot(h, w_ref[:, c2:c2 + ATTN_DIM])
    a_ref[...] = (zglu[:, 0:CONV_CH] * _sigmoid(zglu[:, CONV_CH:c2])).astype(BF16)
    zk = _dot(h, w_ref[:, c2 + ATTN_DIM:c2 + 2 * ATTN_DIM])
    msq_q = head_msq(zq)
    q_ref[...] = (zq * lax.rsqrt(msq_q + EPS) * qg_ref[...] * (HEAD_DIM ** -0.5 * LOG2E)).astype(BF16)
    zvg = _dot(h, w_ref[:, c2 + 2 * ATTN_DIM:])
    msq_k = head_msq(zk)
    k_ref[...] = (zk * lax.rsqrt(msq_k + EPS) * kg_ref[...]).astype(BF16)
    v_ref[...] = zvg[:, 0:ATTN_DIM].astype(BF16)
    g_ref[...] = _sigmoid(zvg[:, ATTN_DIM:] + gb_ref[...]).astype(BF16)


def _in_proj(x2d, norm1_g, w_in, gate_b, q_norm_g, k_norm_g):
    t = x2d.shape[0]
    in_cols = w_in.shape[1]
    idx = jnp.arange(NORM_CHUNK) // HEAD_DIM
    pmat = jnp.where(idx[:, None] == idx[None, :], 1.0 / HEAD_DIM, 0.0).astype(BF16)
    qg = jnp.tile(q_norm_g.astype(F32), N_HEADS).reshape(1, ATTN_DIM)
    kg = jnp.tile(k_norm_g.astype(F32), N_HEADS).reshape(1, ATTN_DIM)
    row = lambda i: (i, 0)
    tok = lambda width, dt: jax.ShapeDtypeStruct((t, width), dt)
    return pl.pallas_call(
        _in_proj_kernel,
        grid=(t // TM_IN,),
        in_specs=[
            pl.BlockSpec((TM_IN, D_MODEL), row),
            _resident((1, D_MODEL)),
            _resident((D_MODEL, in_cols)),
            _resident((1, 2 * D_MODEL)),
            _resident((1, ATTN_DIM)),
            _resident((1, ATTN_DIM)),
            _resident((NORM_CHUNK, NORM_CHUNK)),
        ],
        out_specs=[
            pl.BlockSpec((TM_IN, CONV_CH), row),
            pl.BlockSpec((TM_IN, ATTN_DIM), row),
            pl.BlockSpec((TM_IN, ATTN_DIM), row),
            pl.BlockSpec((TM_IN, ATTN_DIM), row),
            pl.BlockSpec((TM_IN, 2 * D_MODEL), row),
        ],
        out_shape=[tok(CONV_CH, BF16), tok(ATTN_DIM, BF16), tok(ATTN_DIM, BF16),
                   tok(ATTN_DIM, BF16), tok(2 * D_MODEL, BF16)],
        compiler_params=pltpu.CompilerParams(
            dimension_semantics=("parallel",), vmem_limit_bytes=VMEM_LIMIT_BYTES),
        name="in_proj",
    )(x2d, norm1_g.reshape(1, D_MODEL), w_in.astype(BF16), gate_b.reshape(1, 2 * D_MODEL), qg, kg, pmat)


def _attn_kernel(slopes_ref, q_ref, k_ref, v_ref, o_ref, sta, stb, qb, kb, vb, bias, og, lg, *, seq):
    blk = ATTN_BLOCK
    ngroups = len(DILATION_GROUPS)
    dils = [d for _, d in DILATION_GROUPS]
    assert dils[0] == 1 and all(dils[g] == SPLIT * dils[g - 1] for g in range(1, ngroups))
    hp = pl.program_id(1)

    def kv_slot_rows(g):
        sub_len = seq // dils[g]
        return sub_len + (blk if sub_len > blk else 0)

    lane_q = lax.broadcasted_iota(jnp.int32, (blk, LANES), 1)
    first_q = lane_q < HEAD_DIM

    def put(kind, g, row0, val):
        n = val.shape[0]
        if kind == "q":
            qb[g, row0:row0 + n, :] = val.astype(BF16)
        elif kind == "k":
            kb[g, row0:row0 + n, :] = val.astype(BF16)
        else:
            first_v = lax.broadcasted_iota(jnp.int32, val.shape, 1) < HEAD_DIM
            vb[g, 0, row0:row0 + n, :] = jnp.where(first_v, val, 1.0).astype(BF16)
            vb[g, 1, row0:row0 + n, :] = jnp.where(first_v, 1.0, val).astype(BF16)

    for t_ref, kind in ((q_ref, "q"), (k_ref, "k"), (v_ref, "v")):
        cur, nxt = sta, stb
        for g in range(ngroups):
            sub_len = seq // dils[g]
            slot = sub_len if kind == "q" else kv_slot_rows(g)
            front = slot - sub_len
            for j in range(dils[g]):
                if front:
                    put(kind, g, j * slot, jnp.zeros((front, LANES), F32))
            if g == 0:
                val = t_ref[...].astype(F32)
                cur[...] = val
                put(kind, g, front, val)
                continue
            prev_len = seq // dils[g - 1]
            for jp in range(dils[g - 1]):
                for r in range(SPLIT):
                    j = jp * SPLIT + r
                    val = cur[pl.ds(jp * prev_len + r, sub_len, stride=SPLIT), :]
                    if g + 1 < ngroups:
                        nxt[j * sub_len:(j + 1) * sub_len, :] = val
                    put(kind, g, j * slot + front, val)
            cur, nxt = nxt, cur

    rows = lax.broadcasted_iota(jnp.int32, (blk, 2 * blk), 0)
    cols = lax.broadcasted_iota(jnp.int32, (blk, 2 * blk), 1)
    steps2 = rows + blk - cols
    rows1 = lax.broadcasted_iota(jnp.int32, (blk, blk), 0)
    cols1 = lax.broadcasted_iota(jnp.int32, (blk, blk), 1)
    steps1 = rows1 - cols1

    def scores(qt, kt, bias_of_head):
        out = []
        for hh in range(2):
            sel_q = first_q if hh == 0 else ~first_q
            qh = jnp.where(sel_q, qt, jnp.zeros_like(qt))
            s = lax.dot_general(qh, kt, (((1,), (1,)), ((), ())), preferred_element_type=F32)
            out.append(s + bias_of_head(hh))
        return out

    def weighted_values(ss, v_of_head):
        res, mx = [], []
        for hh in range(2):
            m = jnp.max(ss[hh], axis=-1, keepdims=True)
            p = jnp.exp2(ss[hh] - m).astype(BF16)
            res.append(_dot(p, v_of_head(hh)))
            mx.append(m)
        acc = jnp.where(first_q, res[0], res[1])
        den = pltpu.roll(jnp.where(first_q, res[1], res[0]), HEAD_DIM, 1)
        mm = jnp.where(first_q, mx[0], mx[1])
        return acc / den, (mm + jnp.log2(den)) * LN2

    for g, (window, dil) in enumerate(DILATION_GROUPS):
        n_back = window // dil
        sub_len = seq // dil
        nb = sub_len // blk
        slot = kv_slot_rows(g)
        prev_len = seq // dils[g - 1] if g else seq
        if nb == 1:
            dist_a = jnp.where((steps1 >= 0) & (steps1 <= n_back), steps1.astype(F32), MASK_STEPS)
            variants = (dist_a,)
            nk = blk
        else:
            valid = (steps2 >= 0) & (steps2 <= n_back)
            dist_a = jnp.where(valid, steps2.astype(F32), MASK_STEPS)
            dist_b = jnp.where(valid & (cols >= blk), steps2.astype(F32), MASK_STEPS)
            variants = (dist_a, dist_b)
            nk = 2 * blk
        for var, dist in enumerate(variants):
            for hh in range(2):
                bias[2 * var + hh, :, 0:nk] = (slopes_ref[2 * hp + hh] * (-LOG2E * dil)) * dist

        n_blocks = dil * nb
        pending = {}
        for step in range(n_blocks + ATTN_LOOKAHEAD):
            if step < n_blocks:
                j, jb = divmod(step, nb)
                var = 1 if (nb > 1 and jb == 0) else 0
                pending[step] = scores(
                    qb[g, step * blk:(step + 1) * blk, :], kb[g, j * slot + jb * blk:j * slot + jb * blk + nk, :],
                    lambda hh, var=var, nk=nk: bias[2 * var + hh, :, 0:nk])
            u = step - ATTN_LOOKAHEAD
            if u >= 0:
                j, jb = divmod(u, nb)
                k0 = j * slot + jb * blk
                o, l = weighted_values(pending.pop(u), lambda hh, g=g, k0=k0, nk=nk: vb[g, hh, k0:k0 + nk, :])
                if g == 0:
                    dst = pl.ds(u * blk, blk)
                else:
                    jp, r = divmod(j, SPLIT)
                    dst = pl.ds(jp * prev_len + r + SPLIT * jb * blk, blk, stride=SPLIT)
                og[g, dst, :] = o
                lg[g, dst, :] = l

    for res in (og, lg):
        for g in range(2, ngroups):
            for gg in range(g - 1, 0, -1):
                sub_len = seq // dils[gg]
                prev_len = seq // dils[gg - 1]
                sta[...] = res[g]
                for j in range(dils[gg - 1]):
                    for r in range(SPLIT):
                        src0 = (j * SPLIT + r) * sub_len
                        res[g, pl.ds(j * prev_len + r, sub_len, stride=SPLIT), :] = sta[src0:src0 + sub_len, :]

    def mix_body(i, carry):
        r0 = pl.multiple_of(i * blk, blk)
        ls = [lg[g, pl.ds(r0, blk), :] for g in range(ngroups)]
        mmax = functools.reduce(jnp.maximum, ls)
        ws = [jnp.exp(l - mmax) for l in ls]
        wsum = functools.reduce(lambda a, b: a + b, ws)
        acc = ws[0] * og[0, pl.ds(r0, blk), :]
        for g in range(1, ngroups):
            acc = acc + ws[g] * og[g, pl.ds(r0, blk), :]
        o_ref[pl.ds(r0, blk), :] = (acc / wsum).astype(o_ref.dtype)
        return carry

    lax.fori_loop(0, seq // blk, mix_body, 0)


def _attention(q, k, v, batch, seq):
    slopes = 2.0 ** (-8.0 * jnp.arange(1, N_HEADS + 1, dtype=F32) / N_HEADS)
    q3, k3, v3 = (t.reshape(batch, seq, ATTN_DIM) for t in (q, k, v))
    ngroups = len(DILATION_GROUPS)
    kv_rows = max(seq + d * (ATTN_BLOCK if seq // d > ATTN_BLOCK else 0) for _, d in DILATION_GROUPS)
    blk = pl.BlockSpec((None, seq, LANES), lambda b, h: (b, 0, h))
    out = pl.pallas_call(
        functools.partial(_attn_kernel, seq=seq),
        grid=(batch, ATTN_DIM // LANES),
        in_specs=[pl.BlockSpec(memory_space=pltpu.SMEM), blk, blk, blk],
        out_specs=blk,
        out_shape=jax.ShapeDtypeStruct((batch, seq, ATTN_DIM), BF16),
        scratch_shapes=[
            pltpu.VMEM((seq, LANES), F32),
            pltpu.VMEM((seq, LANES), F32),
            pltpu.VMEM((ngroups, seq, LANES), BF16),
            pltpu.VMEM((ngroups, kv_rows, LANES), BF16),
            pltpu.VMEM((ngroups, 2, kv_rows, LANES), BF16),
            pltpu.VMEM((4, ATTN_BLOCK, 2 * ATTN_BLOCK), F32),
            pltpu.VMEM((ngroups, seq, LANES), F32),
            pltpu.VMEM((ngroups, seq, LANES), F32),
        ],
        compiler_params=pltpu.CompilerParams(
            dimension_semantics=("parallel", "parallel"), vmem_limit_bytes=VMEM_LIMIT_BYTES),
        name="attention",
    )(slopes, q3, k3, v3)
    return out.reshape(batch * seq, ATTN_DIM)


def _token_out_kernel(x_ref, a_ref, halo_ref, o_ref, g_ref, cw_ref, cb_ref, cg_ref,
                      wc_ref, wa_ref, wo_ref, g2_ref, x1_ref, h2_ref, abuf, cbuf, ybuf, *, tiles_per_seq):
    tm = TM_OUT
    first = (pl.program_id(0) % tiles_per_seq) == 0
    n_slabs = CONV_CH // LANES
    for c in range(n_slabs):
        lanes = slice(c * LANES, (c + 1) * LANES)
        abuf[c, pl.ds(0, HALO, stride=2), :] = jnp.where(first, 0.0, halo_ref[:, lanes].astype(F32))
        abuf[c, pl.ds(2 * HALO, tm, stride=2), :] = a_ref[:, lanes].astype(F32)

    off = HALO - (CONV_WIDTH - 1)
    yb_cols = D_MODEL // (n_slabs // 2)
    for c in range(n_slabs):
        lanes = slice(c * LANES, (c + 1) * LANES)
        acc = jnp.zeros((tm, LANES), F32) + cb_ref[:, lanes]
        for j in range(CONV_WIDTH):
            acc = acc + cw_ref[j:j + 1, lanes] * abuf[c, pl.ds(2 * (off + j), tm, stride=2), :]
        cbuf[:, lanes] = acc
        if c % 2 == 1:
            n0 = (c // 2) * yb_cols
            ybuf[:, n0:n0 + yb_cols] = _dot(o_ref[...], wa_ref[:, n0:n0 + yb_cols])

    cv = cbuf[...]
    ms = jnp.mean(cv * cv, axis=-1, keepdims=True)
    cn = cv * lax.rsqrt(ms + EPS) * cg_ref[...]
    ac = (cn * _sigmoid(cn)).astype(BF16)

    ya = _dot(ac, wc_ref[...])
    yb = ybuf[...]
    ga = g_ref[:, 0:D_MODEL].astype(F32)
    gb = g_ref[:, D_MODEL:2 * D_MODEL].astype(F32)
    mix = (ga * ya + gb * yb).astype(BF16)
    x1 = x_ref[...] + _dot(mix, wo_ref[...])
    x1_ref[...] = x1
    ms2 = jnp.mean(x1 * x1, axis=-1, keepdims=True)
    h2_ref[...] = (x1 * lax.rsqrt(ms2 + EPS) * g2_ref[...]).astype(BF16)


def _token_out(x2d, a, o, g, conv_w, conv_b, conv_norm_g, w_conv_out, w_attn_out, w_out, norm2_g, seq):
    t = x2d.shape[0]
    tm = TM_OUT
    tiles_per_seq = seq // tm
    halo_per_tile = tm // HALO
    cw = jnp.zeros((HALO, CONV_CH), F32).at[:CONV_WIDTH].set(conv_w.astype(F32))
    row = lambda i: (i, 0)
    return pl.pallas_call(
        functools.partial(_token_out_kernel, tiles_per_seq=tiles_per_seq),
        grid=(t // tm,),
        in_specs=[
            pl.BlockSpec((tm, D_MODEL), row),
            pl.BlockSpec((tm, CONV_CH), row),
            pl.BlockSpec((HALO, CONV_CH), lambda i: (jnp.maximum(i * halo_per_tile - 1, 0), 0)),
            pl.BlockSpec((tm, ATTN_DIM), row),
            pl.BlockSpec((tm, 2 * D_MODEL), row),
            _resident((HALO, CONV_CH)),
            _resident((1, CONV_CH)),
            _resident((1, CONV_CH)),
            _resident((CONV_CH, D_MODEL)),
            _resident((ATTN_DIM, D_MODEL)),
            _resident((D_MODEL, D_MODEL)),
            _resident((1, D_MODEL)),
        ],
        out_specs=[pl.BlockSpec((tm, D_MODEL), row), pl.BlockSpec((tm, D_MODEL), row)],
        out_shape=[jax.ShapeDtypeStruct((t, D_MODEL), F32), jax.ShapeDtypeStruct((t, D_MODEL), BF16)],
        scratch_shapes=[pltpu.VMEM((CONV_CH // LANES, 2 * (HALO + tm), LANES), F32),
                        pltpu.VMEM((tm, CONV_CH), F32), pltpu.VMEM((tm, D_MODEL), F32)],
        compiler_params=pltpu.CompilerParams(
            dimension_semantics=("parallel",), vmem_limit_bytes=VMEM_LIMIT_BYTES),
        name="token_out",
    )(x2d, a, a, o, g, cw, conv_b.reshape(1, CONV_CH), conv_norm_g.reshape(1, CONV_CH),
      w_conv_out.astype(BF16), w_attn_out.astype(BF16), w_out.astype(BF16), norm2_g.reshape(1, D_MODEL))


def _ffn_kernel(x1_ref, h2_ref, wup_ref, fw_ref, fb_ref, wdn_ref, x2_ref, ubuf, carry, act, *, tiles_per_seq):
    tm = TM_FFN
    first = (pl.program_id(0) % tiles_per_seq) == 0

    @pl.when(first)
    def _():
        carry[...] = jnp.zeros_like(carry)

    h2 = h2_ref[...]
    top = SUBLANES
    taps = FFN_CONV_WIDTH

    def conv(col0):
        u = _dot(h2, wup_ref[:, col0:col0 + FFN_CHUNK])
        ubuf[0:top, :] = carry[:, col0:col0 + FFN_CHUNK]
        ubuf[top:top + tm, :] = u
        carry[:, col0:col0 + FFN_CHUNK] = u[tm - top:tm, :]
        out = fb_ref[:, col0:col0 + FFN_CHUNK] + fw_ref[taps - 1:taps, col0:col0 + FFN_CHUNK] * u
        for j in range(taps - 1):
            shift = taps - 1 - j
            out = out + fw_ref[j:j + 1, col0:col0 + FFN_CHUNK] * ubuf[top - shift:top - shift + tm, :]
        return out

    for c in range(0, D_FF, FFN_CHUNK):
        uv = conv(c)
        ug = conv(D_FF + c)
        act[:, c:c + FFN_CHUNK] = (ug * _sigmoid(ug) * uv).astype(BF16)

    x2_ref[...] = x1_ref[...] + _dot(act[...], wdn_ref[...])


def _ffn(x1, h2, w_up, ffn_conv_w, ffn_conv_b, w_down, seq):
    t = x1.shape[0]
    tm = TM_FFN
    fw = jnp.zeros((SUBLANES, 2 * D_FF), F32).at[:FFN_CONV_WIDTH].set(ffn_conv_w.astype(F32))
    row = lambda i: (i, 0)
    return pl.pallas_call(
        functools.partial(_ffn_kernel, tiles_per_seq=seq // tm),
        grid=(t // tm,),
        in_specs=[
            pl.BlockSpec((tm, D_MODEL), row),
            pl.BlockSpec((tm, D_MODEL), row),
            _resident((D_MODEL, 2 * D_FF)),
            _resident((SUBLANES, 2 * D_FF)),
            _resident((1, 2 * D_FF)),
            _resident((D_FF, D_MODEL)),
        ],
        out_specs=pl.BlockSpec((tm, D_MODEL), row),
        out_shape=jax.ShapeDtypeStruct((t, D_MODEL), F32),
        scratch_shapes=[
            pltpu.VMEM((SUBLANES + tm, FFN_CHUNK), F32),
            pltpu.VMEM((SUBLANES, 2 * D_FF), F32),
            pltpu.VMEM((tm, D_FF), BF16),
        ],
        compiler_params=pltpu.CompilerParams(
            dimension_semantics=("arbitrary",), vmem_limit_bytes=VMEM_LIMIT_BYTES),
        name="ffn",
    )(x1, h2, w_up.astype(BF16), fw, ffn_conv_b.reshape(1, 2 * D_FF), w_down.astype(BF16))


def kernel(x, norm1_g, w_in, gate_b, conv_w, conv_b, conv_norm_g, w_conv_out, q_norm_g, k_norm_g,
           w_attn_out, w_out, norm2_g, w_up, ffn_conv_w, ffn_conv_b, w_down):
    batch, seq, d = x.shape
    assert d == D_MODEL and seq % TM_IN == 0 and seq % TM_OUT == 0 and seq % TM_FFN == 0
    assert all(seq % (dil * ATTN_BLOCK) == 0 for _, dil in DILATION_GROUPS)
    depth = norm1_g.shape[0]
    x2d = x.reshape(batch * seq, d)
    for l in range(depth):
        a, q, k, v, g = _in_proj(x2d, norm1_g[l], w_in[l], gate_b[l], q_norm_g[l], k_norm_g[l])
        o = _attention(q, k, v, batch, seq)
        x1, h2 = _token_out(x2d, a, o, g, conv_w[l], conv_b[l], conv_norm_g[l], w_conv_out[l],
                            w_attn_out[l], w_out[l], norm2_g[l], seq)
        x2d = _ffn(x1, h2, w_up[l], ffn_conv_w[l], ffn_conv_b[l], w_down[l], seq)
    return x2d.reshape(batch, seq, d)
```

```python
import functools

import jax
import jax.numpy as jnp
from jax import lax
from jax.experimental import pallas as pl
from jax.experimental.pallas import tpu as pltpu

F32 = jnp.float32
BF16 = jnp.bfloat16

D_MODEL = 1024
CONV_CH = D_MODEL
CONV_WIDTH = 31
N_HEADS = 16
HEAD_DIM = 64
ATTN_DIM = N_HEADS * HEAD_DIM
DILATION_GROUPS = ((128, 1), (512, 4), (2048, 16))
ATTN_BLOCK = 128
D_FF = 2816
FFN_CONV_WIDTH = 3
EPS = 1e-6

LANES = 128
SUBLANES = 8
VMEM_LIMIT_BYTES = 56 * 1024 * 1024
MASK_STEPS = 1e30
LOG2E = 1.4426950408889634

TM_IN = 512
TM_OUT = 256
TM_FFN = 256
HALO = 32
NORM_CHUNK = 256
FFN_CHUNK = 256
ATTN_LOOKAHEAD = 3
SPLIT = 4


def _dot(a, b):
    return jnp.dot(a, b, preferred_element_type=F32)


def _sigmoid(x):
    return 1.0 / (1.0 + jnp.exp(-x))


def _resident(shape):
    return pl.BlockSpec(shape, lambda *_: (0,) * len(shape), pipeline_mode=pl.Buffered(1))


def _in_proj_kernel(x_ref, g1_ref, w_ref, gb_ref, qg_ref, kg_ref, pm_ref,
                    a_ref, q_ref, k_ref, v_ref, g_ref):
    x = x_ref[...]
    ms = jnp.mean(x * x, axis=-1, keepdims=True)
    h = (x * lax.rsqrt(ms + EPS) * g1_ref[...]).astype(BF16)

    tm = x.shape[0]
    c2 = 2 * CONV_CH
    n_chunks = ATTN_DIM // NORM_CHUNK

    def head_msq(z):
        sq = (z * z).astype(BF16)
        stacked = jnp.concatenate([sq[:, c * NORM_CHUNK:(c + 1) * NORM_CHUNK] for c in range(n_chunks)], axis=0)
        msq = _dot(stacked, pm_ref[...])
        return jnp.concatenate([msq[c * tm:(c + 1) * tm, :] for c in range(n_chunks)], axis=1)

    zglu = _dot(h, w_ref[:, 0:c2])
    zq = _dot(h, w_ref[:, c2:c2 + ATTN_DIM])
    a_ref[...] = (zglu[:, 0:CONV_CH] * _sigmoid(zglu[:, CONV_CH:c2])).astype(BF16)
    zk = _dot(h, w_ref[:, c2 + ATTN_DIM:c2 + 2 * ATTN_DIM])
    msq_q = head_msq(zq)
    q_ref[...] = (zq * lax.rsqrt(msq_q + EPS) * qg_ref[...] * (HEAD_DIM ** -0.5 * LOG2E)).astype(BF16)
    zvg = _dot(h, w_ref[:, c2 + 2 * ATTN_DIM:])
    msq_k = head_msq(zk)
    k_ref[...] = (zk * lax.rsqrt(msq_k + EPS) * kg_ref[...]).astype(BF16)
    v_ref[...] = zvg[:, 0:ATTN_DIM].astype(BF16)
    g_ref[...] = _sigmoid(zvg[:, ATTN_DIM:] + gb_ref[...]).astype(BF16)


def _in_proj(x2d, norm1_g, w_in, gate_b, q_norm_g, k_norm_g):
    t = x2d.shape[0]
    in_cols = w_in.shape[1]
    idx = jnp.arange(NORM_CHUNK) // HEAD_DIM
    pmat = jnp.where(idx[:, None] == idx[None, :], 1.0 / HEAD_DIM, 0.0).astype(BF16)
    qg = jnp.tile(q_norm_g.astype(F32), N_HEADS).reshape(1, ATTN_DIM)
    kg = jnp.tile(k_norm_g.astype(F32), N_HEADS).reshape(1, ATTN_DIM)
    row = lambda i: (i, 0)
    tok = lambda width, dt: jax.ShapeDtypeStruct((t, width), dt)
    return pl.pallas_call(
        _in_proj_kernel,
        grid=(t // TM_IN,),
        in_specs=[
            pl.BlockSpec((TM_IN, D_MODEL), row),
            _resident((1, D_MODEL)),
            _resident((D_MODEL, in_cols)),
            _resident((1, 2 * D_MODEL)),
            _resident((1, ATTN_DIM)),
            _resident((1, ATTN_DIM)),
            _resident((NORM_CHUNK, NORM_CHUNK)),
        ],
        out_specs=[
            pl.BlockSpec((TM_IN, CONV_CH), row),
            pl.BlockSpec((TM_IN, ATTN_DIM), row),
            pl.BlockSpec((TM_IN, ATTN_DIM), row),
            pl.BlockSpec((TM_IN, ATTN_DIM), row),
            pl.BlockSpec((TM_IN, 2 * D_MODEL), row),
        ],
        out_shape=[tok(CONV_CH, BF16), tok(ATTN_DIM, BF16), tok(ATTN_DIM, BF16),
                   tok(ATTN_DIM, BF16), tok(2 * D_MODEL, BF16)],
        compiler_params=pltpu.CompilerParams(
            dimension_semantics=("parallel",), vmem_limit_bytes=VMEM_LIMIT_BYTES),
        name="in_proj",
    )(x2d, norm1_g.reshape(1, D_MODEL), w_in.astype(BF16), gate_b.reshape(1, 2 * D_MODEL), qg, kg, pmat)


def _attn_kernel(slopes_ref, q_ref, k_ref, v_ref, o_ref, sta, stb, qb, kb, vb, bias, og, lg, *, seq):
    blk = ATTN_BLOCK
    ngroups = len(DILATION_GROUPS)
    dils = [d for _, d in DILATION_GROUPS]
    assert dils[0] == 1 and all(dils[g] == SPLIT * dils[g - 1] for g in range(1, ngroups))
    hp = pl.program_id(1)

    def kv_slot_rows(g):
        sub_len = seq // dils[g]
        return sub_len + (blk if sub_len > blk else 0)

    lane_q = lax.broadcasted_iota(jnp.int32, (blk, LANES), 1)
    first_q = lane_q < HEAD_DIM

    def put(kind, g, row0, val):
        n = val.shape[0]
        if kind == "q":
            qb[g, row0:row0 + n, :] = val.astype(BF16)
        elif kind == "k":
            kb[g, row0:row0 + n, :] = val.astype(BF16)
        else:
            first_v = lax.broadcasted_iota(jnp.int32, val.shape, 1) < HEAD_DIM
            vb[g, 0, row0:row0 + n, :] = jnp.where(first_v, val, 1.0).astype(BF16)
            vb[g, 1, row0:row0 + n, :] = jnp.where(first_v, 1.0, val).astype(BF16)

    for t_ref, kind in ((q_ref, "q"), (k_ref, "k"), (v_ref, "v")):
        cur, nxt = sta, stb
        for g in range(ngroups):
            sub_len = seq // dils[g]
            slot = sub_len if kind == "q" else kv_slot_rows(g)
            front = slot - sub_len
            for j in range(dils[g]):
                if front:
                    put(kind, g, j * slot, jnp.zeros((front, LANES), F32))
            if g == 0:
                val = t_ref[...].astype(F32)
                cur[...] = val
                put(kind, g, front, val)
                continue
            prev_len = seq // dils[g - 1]
            for jp in range(dils[g - 1]):
                for r in range(SPLIT):
                    j = jp * SPLIT + r
                    val = cur[pl.ds(jp * prev_len + r, sub_len, stride=SPLIT), :]
                    if g + 1 < ngroups:
                        nxt[j * sub_len:(j + 1) * sub_len, :] = val
                    put(kind, g, j * slot + front, val)
            cur, nxt = nxt, cur

    rows = lax.broadcasted_iota(jnp.int32, (blk, 2 * blk), 0)
    cols = lax.broadcasted_iota(jnp.int32, (blk, 2 * blk), 1)
    steps2 = rows + blk - cols
    rows1 = lax.broadcasted_iota(jnp.int32, (blk, blk), 0)
    cols1 = lax.broadcasted_iota(jnp.int32, (blk, blk), 1)
    steps1 = rows1 - cols1

    def scores(qt, kt, bias_of_head):
        out = []
        for hh in range(2):
            sel_q = first_q if hh == 0 else ~first_q
            qh = jnp.where(sel_q, qt, jnp.zeros_like(qt))
            s = lax.dot_general(qh, kt, (((1,), (1,)), ((), ())), preferred_element_type=F32)
            out.append(s + bias_of_head(hh))
        return out

    def weighted_values(ss, v_of_head):
        res, mx = [], []
        for hh in range(2):
            m = jnp.max(ss[hh], axis=-1, keepdims=True)
            p = jnp.exp2(ss[hh] - m).astype(BF16)
            res.append(_dot(p, v_of_head(hh)))
            mx.append(m)
        acc = jnp.where(first_q, res[0], res[1])
        den = pltpu.roll(jnp.where(first_q, res[1], res[0]), HEAD_DIM, 1)
        mm = jnp.where(first_q, mx[0], mx[1])
        return acc / den, mm + jnp.log2(den)

    for g, (window, dil) in enumerate(DILATION_GROUPS):
        n_back = window // dil
        sub_len = seq // dil
        nb = sub_len // blk
        slot = kv_slot_rows(g)
        prev_len = seq // dils[g - 1] if g else seq
        if nb == 1:
            dist_a = jnp.where((steps1 >= 0) & (steps1 <= n_back), steps1.astype(F32), MASK_STEPS)
            variants = (dist_a,)
            nk = blk
        else:
            valid = (steps2 >= 0) & (steps2 <= n_back)
            dist_a = jnp.where(valid, steps2.astype(F32), MASK_STEPS)
            dist_b = jnp.where(valid & (cols >= blk), steps2.astype(F32), MASK_STEPS)
            variants = (dist_a, dist_b)
            nk = 2 * blk
        for var, dist in enumerate(variants):
            for hh in range(2):
                bias[2 * var + hh, :, 0:nk] = (slopes_ref[2 * hp + hh] * (-LOG2E * dil)) * dist

        n_blocks = dil * nb
        pending = {}
        for step in range(n_blocks + ATTN_LOOKAHEAD):
            if step < n_blocks:
                j, jb = divmod(step, nb)
                var = 1 if (nb > 1 and jb == 0) else 0
                pending[step] = scores(
                    qb[g, step * blk:(step + 1) * blk, :], kb[g, j * slot + jb * blk:j * slot + jb * blk + nk, :],
                    lambda hh, var=var, nk=nk: bias[2 * var + hh, :, 0:nk])
            u = step - ATTN_LOOKAHEAD
            if u >= 0:
                j, jb = divmod(u, nb)
                k0 = j * slot + jb * blk
                o, l = weighted_values(pending.pop(u), lambda hh, g=g, k0=k0, nk=nk: vb[g, hh, k0:k0 + nk, :])
                if g == 0:
                    dst = pl.ds(u * blk, blk)
                else:
                    jp, r = divmod(j, SPLIT)
                    dst = pl.ds(jp * prev_len + r + SPLIT * jb * blk, blk, stride=SPLIT)
                og[g, dst, :] = o
                lg[g, dst, :] = l

    for res in (og, lg):
        for g in range(2, ngroups):
            for gg in range(g - 1, 0, -1):
                sub_len = seq // dils[gg]
                prev_len = seq // dils[gg - 1]
                sta[...] = res[g]
                for j in range(dils[gg - 1]):
                    for r in range(SPLIT):
                        src0 = (j * SPLIT + r) * sub_len
                        res[g, pl.ds(j * prev_len + r, sub_len, stride=SPLIT), :] = sta[src0:src0 + sub_len, :]

    def mix_body(i, carry):
        r0 = pl.multiple_of(i * blk, blk)
        ls = [lg[g, pl.ds(r0, blk), :] for g in range(ngroups)]
        mmax = functools.reduce(jnp.maximum, ls)
        ws = [jnp.exp2(l - mmax) for l in ls]
        wsum = functools.reduce(lambda a, b: a + b, ws)
        acc = ws[0] * og[0, pl.ds(r0, blk), :]
        for g in range(1, ngroups):
            acc = acc + ws[g] * og[g, pl.ds(r0, blk), :]
        o_ref[pl.ds(r0, blk), :] = (acc / wsum).astype(o_ref.dtype)
        return carry

    lax.fori_loop(0, seq // blk, mix_body, 0)


def _attention(q, k, v, batch, seq):
    slopes = 2.0 ** (-8.0 * jnp.arange(1, N_HEADS + 1, dtype=F32) / N_HEADS)
    q3, k3, v3 = (t.reshape(batch, seq, ATTN_DIM) for t in (q, k, v))
    ngroups = len(DILATION_GROUPS)
    kv_rows = max(seq + d * (ATTN_BLOCK if seq // d > ATTN_BLOCK else 0) for _, d in DILATION_GROUPS)
    blk = pl.BlockSpec((None, seq, LANES), lambda b, h: (b, 0, h))
    out = pl.pallas_call(
        functools.partial(_attn_kernel, seq=seq),
        grid=(batch, ATTN_DIM // LANES),
        in_specs=[pl.BlockSpec(memory_space=pltpu.SMEM), blk, blk, blk],
        out_specs=blk,
        out_shape=jax.ShapeDtypeStruct((batch, seq, ATTN_DIM), BF16),
        scratch_shapes=[
            pltpu.VMEM((seq, LANES), F32),
            pltpu.VMEM((seq, LANES), F32),
            pltpu.VMEM((ngroups, seq, LANES), BF16),
            pltpu.VMEM((ngroups, kv_rows, LANES), BF16),
            pltpu.VMEM((ngroups, 2, kv_rows, LANES), BF16),
            pltpu.VMEM((4, ATTN_BLOCK, 2 * ATTN_BLOCK), F32),
            pltpu.VMEM((ngroups, seq, LANES), F32),
            pltpu.VMEM((ngroups, seq, LANES), F32),
        ],
        compiler_params=pltpu.CompilerParams(
            dimension_semantics=("parallel", "parallel"), vmem_limit_bytes=VMEM_LIMIT_BYTES),
        name="attention",
    )(slopes, q3, k3, v3)
    return out.reshape(batch * seq, ATTN_DIM)


def _token_out_kernel(x_ref, a_ref, halo_ref, o_ref, g_ref, cw_ref, cb_ref, cg_ref,
                      wc_ref, wa_ref, wo_ref, g2_ref, x1_ref, h2_ref, abuf, cbuf, ybuf, *, tiles_per_seq):
    tm = TM_OUT
    first = (pl.program_id(0) % tiles_per_seq) == 0
    n_slabs = CONV_CH // LANES
    for c in range(n_slabs):
        lanes = slice(c * LANES, (c + 1) * LANES)
        abuf[c, pl.ds(0, HALO, stride=2), :] = jnp.where(first, 0.0, halo_ref[:, lanes].astype(F32))
        abuf[c, pl.ds(2 * HALO, tm, stride=2), :] = a_ref[:, lanes].astype(F32)

    off = HALO - (CONV_WIDTH - 1)
    yb_cols = D_MODEL // (n_slabs // 2)
    for c in range(n_slabs):
        lanes = slice(c * LANES, (c + 1) * LANES)
        acc = jnp.zeros((tm, LANES), F32) + cb_ref[:, lanes]
        for j in range(CONV_WIDTH):
            acc = acc + cw_ref[j:j + 1, lanes] * abuf[c, pl.ds(2 * (off + j), tm, stride=2), :]
        cbuf[:, lanes] = acc
        if c % 2 == 1:
            n0 = (c // 2) * yb_cols
            ybuf[:, n0:n0 + yb_cols] = _dot(o_ref[...], wa_ref[:, n0:n0 + yb_cols])

    cv = cbuf[...]
    ms = jnp.mean(cv * cv, axis=-1, keepdims=True)
    cn = cv * lax.rsqrt(ms + EPS) * cg_ref[...]
    ac = (cn * _sigmoid(cn)).astype(BF16)

    ya = _dot(ac, wc_ref[...])
    yb = ybuf[...]
    ga = g_ref[:, 0:D_MODEL].astype(F32)
    gb = g_ref[:, D_MODEL:2 * D_MODEL].astype(F32)
    mix = (ga * ya + gb * yb).astype(BF16)
    x1 = x_ref[...] + _dot(mix, wo_ref[...])
    x1_ref[...] = x1
    ms2 = jnp.mean(x1 * x1, axis=-1, keepdims=True)
    h2_ref[...] = (x1 * lax.rsqrt(ms2 + EPS) * g2_ref[...]).astype(BF16)


def _token_out(x2d, a, o, g, conv_w, conv_b, conv_norm_g, w_conv_out, w_attn_out, w_out, norm2_g, seq):
    t = x2d.shape[0]
    tm = TM_OUT
    tiles_per_seq = seq // tm
    halo_per_tile = tm // HALO
    cw = jnp.zeros((HALO, CONV_CH), F32).at[:CONV_WIDTH].set(conv_w.astype(F32))
    row = lambda i: (i, 0)
    return pl.pallas_call(
        functools.partial(_token_out_kernel, tiles_per_seq=tiles_per_seq),
        grid=(t // tm,),
        in_specs=[
            pl.BlockSpec((tm, D_MODEL), row),
            pl.BlockSpec((tm, CONV_CH), row),
            pl.BlockSpec((HALO, CONV_CH), lambda i: (jnp.maximum(i * halo_per_tile - 1, 0), 0)),
            pl.BlockSpec((tm, ATTN_DIM), row),
            pl.BlockSpec((tm, 2 * D_MODEL), row),
            _resident((HALO, CONV_CH)),
            _resident((1, CONV_CH)),
            _resident((1, CONV_CH)),
            _resident((CONV_CH, D_MODEL)),
            _resident((ATTN_DIM, D_MODEL)),
            _resident((D_MODEL, D_MODEL)),
            _resident((1, D_MODEL)),
        ],
        out_specs=[pl.BlockSpec((tm, D_MODEL), row), pl.BlockSpec((tm, D_MODEL), row)],
        out_shape=[jax.ShapeDtypeStruct((t, D_MODEL), F32), jax.ShapeDtypeStruct((t, D_MODEL), BF16)],
        scratch_shapes=[pltpu.VMEM((CONV_CH // LANES, 2 * (HALO + tm), LANES), F32),
                        pltpu.VMEM((tm, CONV_CH), F32), pltpu.VMEM((tm, D_MODEL), F32)],
        compiler_params=pltpu.CompilerParams(
            dimension_semantics=("parallel",), vmem_limit_bytes=VMEM_LIMIT_BYTES),
        name="token_out",
    )(x2d, a, a, o, g, cw, conv_b.reshape(1, CONV_CH), conv_norm_g.reshape(1, CONV_CH),
      w_conv_out.astype(BF16), w_attn_out.astype(BF16), w_out.astype(BF16), norm2_g.reshape(1, D_MODEL))


def _ffn_kernel(x1_ref, h2_ref, wup_ref, fw_ref, fb_ref, wdn_ref, x2_ref, ubuf, carry, act, *, tiles_per_seq):
    tm = TM_FFN
    first = (pl.program_id(0) % tiles_per_seq) == 0

    @pl.when(first)
    def _():
        carry[...] = jnp.zeros_like(carry)

    h2 = h2_ref[...]
    top = SUBLANES
    taps = FFN_CONV_WIDTH

    def conv(col0):
        u = _dot(h2, wup_ref[:, col0:col0 + FFN_CHUNK])
        ubuf[0:top, :] = carry[:, col0:col0 + FFN_CHUNK]
        ubuf[top:top + tm, :] = u
        carry[:, col0:col0 + FFN_CHUNK] = u[tm - top:tm, :]
        out = fb_ref[:, col0:col0 + FFN_CHUNK] + fw_ref[taps - 1:taps, col0:col0 + FFN_CHUNK] * u
        for j in range(taps - 1):
            shift = taps - 1 - j
            out = out + fw_ref[j:j + 1, col0:col0 + FFN_CHUNK] * ubuf[top - shift:top - shift + tm, :]
        return out

    for c in range(0, D_FF, FFN_CHUNK):
        uv = conv(c)
        ug = conv(D_FF + c)
        act[:, c:c + FFN_CHUNK] = (ug * _sigmoid(ug) * uv).astype(BF16)

    x2_ref[...] = x1_ref[...] + _dot(act[...], wdn_ref[...])


def _ffn(x1, h2, w_up, ffn_conv_w, ffn_conv_b, w_down, seq):
    t = x1.shape[0]
    tm = TM_FFN
    fw = jnp.zeros((SUBLANES, 2 * D_FF), F32).at[:FFN_CONV_WIDTH].set(ffn_conv_w.astype(F32))
    row = lambda i: (i, 0)
    return pl.pallas_call(
        functools.partial(_ffn_kernel, tiles_per_seq=seq // tm),
        grid=(t // tm,),
        in_specs=[
            pl.BlockSpec((tm, D_MODEL), row),
            pl.BlockSpec((tm, D_MODEL), row),
            _resident((D_MODEL, 2 * D_FF)),
            _resident((SUBLANES, 2 * D_FF)),
            _resident((1, 2 * D_FF)),
            _resident((D_FF, D_MODEL)),
        ],
        out_specs=pl.BlockSpec((tm, D_MODEL), row),
        out_shape=jax.ShapeDtypeStruct((t, D_MODEL), F32),
        scratch_shapes=[
            pltpu.VMEM((SUBLANES + tm, FFN_CHUNK), F32),
            pltpu.VMEM((SUBLANES, 2 * D_FF), F32),
            pltpu.VMEM((tm, D_FF), BF16),
        ],
        compiler_params=pltpu.CompilerParams(
            dimension_semantics=("arbitrary",), vmem_limit_bytes=VMEM_LIMIT_BYTES),
        name="ffn",
    )(x1, h2, w_up.astype(BF16), fw, ffn_conv_b.reshape(1, 2 * D_FF), w_down.astype(BF16))


def kernel(x, norm1_g, w_in, gate_b, conv_w, conv_b, conv_norm_g, w_conv_out, q_norm_g, k_norm_g,
           w_attn_out, w_out, norm2_g, w_up, ffn_conv_w, ffn_conv_b, w_down):
    batch, seq, d = x.shape
    assert d == D_MODEL and seq % TM_IN == 0 and seq % TM_OUT == 0 and seq % TM_FFN == 0
    assert all(seq % (dil * ATTN_BLOCK) == 0 for _, dil in DILATION_GROUPS)
    depth = norm1_g.shape[0]
    x2d = x.reshape(batch * seq, d)
    for l in range(depth):
        a, q, k, v, g = _in_proj(x2d, norm1_g[l], w_in[l], gate_b[l], q_norm_g[l], k_norm_g[l])
        o = _attention(q, k, v, batch, seq)
        x1, h2 = _token_out(x2d, a, o, g, conv_w[l], conv_b[l], conv_norm_g[l], w_conv_out[l],
                            w_attn_out[l], w_out[l], norm2_g[l], seq)
        x2d = _ffn(x1, h2, w_up[l], ffn_conv_w[l], ffn_conv_b[l], w_down[l], seq)
    return x2d.reshape(batch, seq, d)
```

```python
import functools

import jax
import jax.numpy as jnp
from jax import lax
from jax.experimental import pallas as pl
from jax.experimental.pallas import tpu as pltpu

F32 = jnp.float32
BF16 = jnp.bfloat16

D_MODEL = 1024
CONV_CH = D_MODEL
CONV_WIDTH = 31
N_HEADS = 16
HEAD_DIM = 64
ATTN_DIM = N_HEADS * HEAD_DIM
DILATION_GROUPS = ((128, 1), (512, 4), (2048, 16))
ATTN_BLOCK = 128
D_FF = 2816
FFN_CONV_WIDTH = 3
EPS = 1e-6

LANES = 128
SUBLANES = 8
VMEM_LIMIT_BYTES = 56 * 1024 * 1024
MASK_STEPS = 1e30
LOG2E = 1.4426950408889634

TM_IN = 512
TM_OUT = 256
TM_FFN = 512
HALO = 32
NORM_CHUNK = 256
FFN_CHUNK = 256
ATTN_LOOKAHEAD = 3
SPLIT = 4


def _dot(a, b):
    return jnp.dot(a, b, preferred_element_type=F32)


def _sigmoid(x):
    return 1.0 / (1.0 + jnp.exp(-x))


def _resident(shape):
    return pl.BlockSpec(shape, lambda *_: (0,) * len(shape), pipeline_mode=pl.Buffered(1))


def _in_proj_kernel(x_ref, g1_ref, w_ref, gb_ref, qg_ref, kg_ref, pm_ref,
                    a_ref, q_ref, k_ref, v_ref, g_ref):
    x = x_ref[...]
    ms = jnp.mean(x * x, axis=-1, keepdims=True)
    h = (x * lax.rsqrt(ms + EPS) * g1_ref[...]).astype(BF16)

    tm = x.shape[0]
    c2 = 2 * CONV_CH
    n_chunks = ATTN_DIM // NORM_CHUNK

    def head_msq(z):
        sq = (z * z).astype(BF16)
        stacked = jnp.concatenate([sq[:, c * NORM_CHUNK:(c + 1) * NORM_CHUNK] for c in range(n_chunks)], axis=0)
        msq = _dot(stacked, pm_ref[...])
        return jnp.concatenate([msq[c * tm:(c + 1) * tm, :] for c in range(n_chunks)], axis=1)

    zglu = _dot(h, w_ref[:, 0:c2])
    zq = _dot(h, w_ref[:, c2:c2 + ATTN_DIM])
    a_ref[...] = (zglu[:, 0:CONV_CH] * _sigmoid(zglu[:, CONV_CH:c2])).astype(BF16)
    zk = _dot(h, w_ref[:, c2 + ATTN_DIM:c2 + 2 * ATTN_DIM])
    msq_q = head_msq(zq)
    q_ref[...] = (zq * lax.rsqrt(msq_q + EPS) * qg_ref[...] * (HEAD_DIM ** -0.5 * LOG2E)).astype(BF16)
    zvg = _dot(h, w_ref[:, c2 + 2 * ATTN_DIM:])
    msq_k = head_msq(zk)
    k_ref[...] = (zk * lax.rsqrt(msq_k + EPS) * kg_ref[...]).astype(BF16)
    v_ref[...] = zvg[:, 0:ATTN_DIM].astype(BF16)
    g_ref[...] = _sigmoid(zvg[:, ATTN_DIM:] + gb_ref[...]).astype(BF16)


def _in_proj(x2d, norm1_g, w_in, gate_b, q_norm_g, k_norm_g):
    t = x2d.shape[0]
    in_cols = w_in.shape[1]
    idx = jnp.arange(NORM_CHUNK) // HEAD_DIM
    pmat = jnp.where(idx[:, None] == idx[None, :], 1.0 / HEAD_DIM, 0.0).astype(BF16)
    qg = jnp.tile(q_norm_g.astype(F32), N_HEADS).reshape(1, ATTN_DIM)
    kg = jnp.tile(k_norm_g.astype(F32), N_HEADS).reshape(1, ATTN_DIM)
    row = lambda i: (i, 0)
    tok = lambda width, dt: jax.ShapeDtypeStruct((t, width), dt)
    return pl.pallas_call(
        _in_proj_kernel,
        grid=(t // TM_IN,),
        in_specs=[
            pl.BlockSpec((TM_IN, D_MODEL), row),
            _resident((1, D_MODEL)),
            _resident((D_MODEL, in_cols)),
            _resident((1, 2 * D_MODEL)),
            _resident((1, ATTN_DIM)),
            _resident((1, ATTN_DIM)),
            _resident((NORM_CHUNK, NORM_CHUNK)),
        ],
        out_specs=[
            pl.BlockSpec((TM_IN, CONV_CH), row),
            pl.BlockSpec((TM_IN, ATTN_DIM), row),
            pl.BlockSpec((TM_IN, ATTN_DIM), row),
            pl.BlockSpec((TM_IN, ATTN_DIM), row),
            pl.BlockSpec((TM_IN, 2 * D_MODEL), row),
        ],
        out_shape=[tok(CONV_CH, BF16), tok(ATTN_DIM, BF16), tok(ATTN_DIM, BF16),
                   tok(ATTN_DIM, BF16), tok(2 * D_MODEL, BF16)],
        compiler_params=pltpu.CompilerParams(
            dimension_semantics=("parallel",), vmem_limit_bytes=VMEM_LIMIT_BYTES),
        name="in_proj",
    )(x2d, norm1_g.reshape(1, D_MODEL), w_in.astype(BF16), gate_b.reshape(1, 2 * D_MODEL), qg, kg, pmat)


def _attn_kernel(slopes_ref, q_ref, k_ref, v_ref, o_ref, sta, stb, qb, kb, vb, bias, og, lg, *, seq):
    blk = ATTN_BLOCK
    ngroups = len(DILATION_GROUPS)
    dils = [d for _, d in DILATION_GROUPS]
    assert dils[0] == 1 and all(dils[g] == SPLIT * dils[g - 1] for g in range(1, ngroups))
    hp = pl.program_id(1)

    def kv_slot_rows(g):
        sub_len = seq // dils[g]
        return sub_len + (blk if sub_len > blk else 0)

    lane_q = lax.broadcasted_iota(jnp.int32, (blk, LANES), 1)
    first_q = lane_q < HEAD_DIM

    def put(kind, g, row0, val):
        n = val.shape[0]
        if kind == "q":
            qb[g, row0:row0 + n, :] = val.astype(BF16)
        elif kind == "k":
            kb[g, row0:row0 + n, :] = val.astype(BF16)
        else:
            first_v = lax.broadcasted_iota(jnp.int32, val.shape, 1) < HEAD_DIM
            vb[g, 0, row0:row0 + n, :] = jnp.where(first_v, val, 1.0).astype(BF16)
            vb[g, 1, row0:row0 + n, :] = jnp.where(first_v, 1.0, val).astype(BF16)

    for t_ref, kind in ((q_ref, "q"), (k_ref, "k"), (v_ref, "v")):
        cur, nxt = sta, stb
        for g in range(ngroups):
            sub_len = seq // dils[g]
            slot = sub_len if kind == "q" else kv_slot_rows(g)
            front = slot - sub_len
            for j in range(dils[g]):
                if front:
                    put(kind, g, j * slot, jnp.zeros((front, LANES), F32))
            if g == 0:
                val = t_ref[...].astype(F32)
                cur[...] = val
                put(kind, g, front, val)
                continue
            prev_len = seq // dils[g - 1]
            for jp in range(dils[g - 1]):
                for r in range(SPLIT):
                    j = jp * SPLIT + r
                    val = cur[pl.ds(jp * prev_len + r, sub_len, stride=SPLIT), :]
                    if g + 1 < ngroups:
                        nxt[j * sub_len:(j + 1) * sub_len, :] = val
                    put(kind, g, j * slot + front, val)
            cur, nxt = nxt, cur

    rows = lax.broadcasted_iota(jnp.int32, (blk, 2 * blk), 0)
    cols = lax.broadcasted_iota(jnp.int32, (blk, 2 * blk), 1)
    steps2 = rows + blk - cols
    rows1 = lax.broadcasted_iota(jnp.int32, (blk, blk), 0)
    cols1 = lax.broadcasted_iota(jnp.int32, (blk, blk), 1)
    steps1 = rows1 - cols1

    def scores(qt, kt, bias_of_head):
        out = []
        for hh in range(2):
            sel_q = first_q if hh == 0 else ~first_q
            qh = jnp.where(sel_q, qt, jnp.zeros_like(qt))
            s = lax.dot_general(qh, kt, (((1,), (1,)), ((), ())), preferred_element_type=F32)
            out.append(s + bias_of_head(hh))
        return out

    def weighted_values(ss, v_of_head):
        res, mx = [], []
        for hh in range(2):
            m = jnp.max(ss[hh], axis=-1, keepdims=True)
            p = jnp.exp2(ss[hh] - m).astype(BF16)
            res.append(_dot(p, v_of_head(hh)))
            mx.append(m)
        acc = jnp.where(first_q, res[0], res[1])
        den = pltpu.roll(jnp.where(first_q, res[1], res[0]), HEAD_DIM, 1)
        mm = jnp.where(first_q, mx[0], mx[1])
        return acc / den, mm + jnp.log2(den)

    for g, (window, dil) in enumerate(DILATION_GROUPS):
        n_back = window // dil
        sub_len = seq // dil
        nb = sub_len // blk
        slot = kv_slot_rows(g)
        prev_len = seq // dils[g - 1] if g else seq
        if nb == 1:
            dist_a = jnp.where((steps1 >= 0) & (steps1 <= n_back), steps1.astype(F32), MASK_STEPS)
            variants = (dist_a,)
            nk = blk
        else:
            valid = (steps2 >= 0) & (steps2 <= n_back)
            dist_a = jnp.where(valid, steps2.astype(F32), MASK_STEPS)
            dist_b = jnp.where(valid & (cols >= blk), steps2.astype(F32), MASK_STEPS)
            variants = (dist_a, dist_b)
            nk = 2 * blk
        for var, dist in enumerate(variants):
            for hh in range(2):
                bias[2 * var + hh, :, 0:nk] = (slopes_ref[2 * hp + hh] * (-LOG2E * dil)) * dist

        n_blocks = dil * nb
        pending = {}
        for step in range(n_blocks + ATTN_LOOKAHEAD):
            if step < n_blocks:
                j, jb = divmod(step, nb)
                var = 1 if (nb > 1 and jb == 0) else 0
                pending[step] = scores(
                    qb[g, step * blk:(step + 1) * blk, :], kb[g, j * slot + jb * blk:j * slot + jb * blk + nk, :],
                    lambda hh, var=var, nk=nk: bias[2 * var + hh, :, 0:nk])
            u = step - ATTN_LOOKAHEAD
            if u >= 0:
                j, jb = divmod(u, nb)
                k0 = j * slot + jb * blk
                o, l = weighted_values(pending.pop(u), lambda hh, g=g, k0=k0, nk=nk: vb[g, hh, k0:k0 + nk, :])
                if g == 0:
                    dst = pl.ds(u * blk, blk)
                else:
                    jp, r = divmod(j, SPLIT)
                    dst = pl.ds(jp * prev_len + r + SPLIT * jb * blk, blk, stride=SPLIT)
                og[g, dst, :] = o
                lg[g, dst, :] = l

    for res in (og, lg):
        for g in range(2, ngroups):
            for gg in range(g - 1, 0, -1):
                sub_len = seq // dils[gg]
                prev_len = seq // dils[gg - 1]
                sta[...] = res[g]
                for j in range(dils[gg - 1]):
                    for r in range(SPLIT):
                        src0 = (j * SPLIT + r) * sub_len
                        res[g, pl.ds(j * prev_len + r, sub_len, stride=SPLIT), :] = sta[src0:src0 + sub_len, :]

    for i in range(seq // blk):
        r0 = i * blk
        ls = [lg[g, pl.ds(r0, blk), :] for g in range(ngroups)]
        mmax = functools.reduce(jnp.maximum, ls)
        ws = [jnp.exp2(l - mmax) for l in ls]
        wsum = functools.reduce(lambda a, b: a + b, ws)
        acc = ws[0] * og[0, pl.ds(r0, blk), :]
        for g in range(1, ngroups):
            acc = acc + ws[g] * og[g, pl.ds(r0, blk), :]
        o_ref[pl.ds(r0, blk), :] = (acc / wsum).astype(o_ref.dtype)


def _attention(q, k, v, batch, seq):
    slopes = 2.0 ** (-8.0 * jnp.arange(1, N_HEADS + 1, dtype=F32) / N_HEADS)
    q3, k3, v3 = (t.reshape(batch, seq, ATTN_DIM) for t in (q, k, v))
    ngroups = len(DILATION_GROUPS)
    kv_rows = max(seq + d * (ATTN_BLOCK if seq // d > ATTN_BLOCK else 0) for _, d in DILATION_GROUPS)
    blk = pl.BlockSpec((None, seq, LANES), lambda b, h: (b, 0, h))
    out = pl.pallas_call(
        functools.partial(_attn_kernel, seq=seq),
        grid=(batch, ATTN_DIM // LANES),
        in_specs=[pl.BlockSpec(memory_space=pltpu.SMEM), blk, blk, blk],
        out_specs=blk,
        out_shape=jax.ShapeDtypeStruct((batch, seq, ATTN_DIM), BF16),
        scratch_shapes=[
            pltpu.VMEM((seq, LANES), F32),
            pltpu.VMEM((seq, LANES), F32),
            pltpu.VMEM((ngroups, seq, LANES), BF16),
            pltpu.VMEM((ngroups, kv_rows, LANES), BF16),
            pltpu.VMEM((ngroups, 2, kv_rows, LANES), BF16),
            pltpu.VMEM((4, ATTN_BLOCK, 2 * ATTN_BLOCK), F32),
            pltpu.VMEM((ngroups, seq, LANES), F32),
            pltpu.VMEM((ngroups, seq, LANES), F32),
        ],
        compiler_params=pltpu.CompilerParams(
            dimension_semantics=("parallel", "parallel"), vmem_limit_bytes=VMEM_LIMIT_BYTES),
        name="attention",
    )(slopes, q3, k3, v3)
    return out.reshape(batch * seq, ATTN_DIM)


def _token_out_kernel(x_ref, a_ref, halo_ref, o_ref, g_ref, cw_ref, cb_ref, cg_ref,
                      wc_ref, wa_ref, wo_ref, g2_ref, x1_ref, h2_ref, abuf, cbuf, ybuf, *, tiles_per_seq):
    tm = TM_OUT
    first = (pl.program_id(0) % tiles_per_seq) == 0
    n_slabs = CONV_CH // LANES
    for c in range(n_slabs):
        lanes = slice(c * LANES, (c + 1) * LANES)
        abuf[c, pl.ds(0, HALO, stride=2), :] = jnp.where(first, 0.0, halo_ref[:, lanes].astype(F32))
        abuf[c, pl.ds(2 * HALO, tm, stride=2), :] = a_ref[:, lanes].astype(F32)

    off = HALO - (CONV_WIDTH - 1)
    yb_cols = D_MODEL // (n_slabs // 2)
    for c in range(n_slabs):
        lanes = slice(c * LANES, (c + 1) * LANES)
        acc = jnp.zeros((tm, LANES), F32) + cb_ref[:, lanes]
        for j in range(CONV_WIDTH):
            acc = acc + cw_ref[j:j + 1, lanes] * abuf[c, pl.ds(2 * (off + j), tm, stride=2), :]
        cbuf[:, lanes] = acc
        if c % 2 == 1:
            n0 = (c // 2) * yb_cols
            ybuf[:, n0:n0 + yb_cols] = _dot(o_ref[...], wa_ref[:, n0:n0 + yb_cols])

    cv = cbuf[...]
    ms = jnp.mean(cv * cv, axis=-1, keepdims=True)
    cn = cv * lax.rsqrt(ms + EPS) * cg_ref[...]
    ac = (cn * _sigmoid(cn)).astype(BF16)

    ya = _dot(ac, wc_ref[...])
    yb = ybuf[...]
    ga = g_ref[:, 0:D_MODEL].astype(F32)
    gb = g_ref[:, D_MODEL:2 * D_MODEL].astype(F32)
    mix = (ga * ya + gb * yb).astype(BF16)
    x1 = x_ref[...] + _dot(mix, wo_ref[...])
    x1_ref[...] = x1
    ms2 = jnp.mean(x1 * x1, axis=-1, keepdims=True)
    h2_ref[...] = (x1 * lax.rsqrt(ms2 + EPS) * g2_ref[...]).astype(BF16)


def _token_out(x2d, a, o, g, conv_w, conv_b, conv_norm_g, w_conv_out, w_attn_out, w_out, norm2_g, seq):
    t = x2d.shape[0]
    tm = TM_OUT
    tiles_per_seq = seq // tm
    halo_per_tile = tm // HALO
    cw = jnp.zeros((HALO, CONV_CH), F32).at[:CONV_WIDTH].set(conv_w.astype(F32))
    row = lambda i: (i, 0)
    return pl.pallas_call(
        functools.partial(_token_out_kernel, tiles_per_seq=tiles_per_seq),
        grid=(t // tm,),
        in_specs=[
            pl.BlockSpec((tm, D_MODEL), row),
            pl.BlockSpec((tm, CONV_CH), row),
            pl.BlockSpec((HALO, CONV_CH), lambda i: (jnp.maximum(i * halo_per_tile - 1, 0), 0)),
            pl.BlockSpec((tm, ATTN_DIM), row),
            pl.BlockSpec((tm, 2 * D_MODEL), row),
            _resident((HALO, CONV_CH)),
            _resident((1, CONV_CH)),
            _resident((1, CONV_CH)),
            _resident((CONV_CH, D_MODEL)),
            _resident((ATTN_DIM, D_MODEL)),
            _resident((D_MODEL, D_MODEL)),
            _resident((1, D_MODEL)),
        ],
        out_specs=[pl.BlockSpec((tm, D_MODEL), row), pl.BlockSpec((tm, D_MODEL), row)],
        out_shape=[jax.ShapeDtypeStruct((t, D_MODEL), F32), jax.ShapeDtypeStruct((t, D_MODEL), BF16)],
        scratch_shapes=[pltpu.VMEM((CONV_CH // LANES, 2 * (HALO + tm), LANES), F32),
                        pltpu.VMEM((tm, CONV_CH), F32), pltpu.VMEM((tm, D_MODEL), F32)],
        compiler_params=pltpu.CompilerParams(
            dimension_semantics=("parallel",), vmem_limit_bytes=VMEM_LIMIT_BYTES),
        name="token_out",
    )(x2d, a, a, o, g, cw, conv_b.reshape(1, CONV_CH), conv_norm_g.reshape(1, CONV_CH),
      w_conv_out.astype(BF16), w_attn_out.astype(BF16), w_out.astype(BF16), norm2_g.reshape(1, D_MODEL))


def _ffn_kernel(x1_ref, h2_ref, wup_ref, fw_ref, fb_ref, wdn_ref, x2_ref, ubuf, carry, act, *, tiles_per_seq):
    tm = TM_FFN
    first = (pl.program_id(0) % tiles_per_seq) == 0

    @pl.when(first)
    def _():
        carry[...] = jnp.zeros_like(carry)

    h2 = h2_ref[...]
    top = SUBLANES
    taps = FFN_CONV_WIDTH

    def conv(col0):
        u = _dot(h2, wup_ref[:, col0:col0 + FFN_CHUNK])
        ubuf[0:top, :] = carry[:, col0:col0 + FFN_CHUNK]
        ubuf[top:top + tm, :] = u
        carry[:, col0:col0 + FFN_CHUNK] = u[tm - top:tm, :]
        out = fb_ref[:, col0:col0 + FFN_CHUNK] + fw_ref[taps - 1:taps, col0:col0 + FFN_CHUNK] * u
        for j in range(taps - 1):
            shift = taps - 1 - j
            out = out + fw_ref[j:j + 1, col0:col0 + FFN_CHUNK] * ubuf[top - shift:top - shift + tm, :]
        return out

    for c in range(0, D_FF, FFN_CHUNK):
        uv = conv(c)
        ug = conv(D_FF + c)
        act[:, c:c + FFN_CHUNK] = (ug * _sigmoid(ug) * uv).astype(BF16)

    x2_ref[...] = x1_ref[...] + _dot(act[...], wdn_ref[...])


def _ffn(x1, h2, w_up, ffn_conv_w, ffn_conv_b, w_down, seq):
    t = x1.shape[0]
    tm = TM_FFN
    fw = jnp.zeros((SUBLANES, 2 * D_FF), F32).at[:FFN_CONV_WIDTH].set(ffn_conv_w.astype(F32))
    row = lambda i: (i, 0)
    return pl.pallas_call(
        functools.partial(_ffn_kernel, tiles_per_seq=seq // tm),
        grid=(t // tm,),
        in_specs=[
            pl.BlockSpec((tm, D_MODEL), row),
            pl.BlockSpec((tm, D_MODEL), row),
            _resident((D_MODEL, 2 * D_FF)),
            _resident((SUBLANES, 2 * D_FF)),
            _resident((1, 2 * D_FF)),
            _resident((D_FF, D_MODEL)),
        ],
        out_specs=pl.BlockSpec((tm, D_MODEL), row),
        out_shape=jax.ShapeDtypeStruct((t, D_MODEL), F32),
        scratch_shapes=[
            pltpu.VMEM((SUBLANES + tm, FFN_CHUNK), F32),
            pltpu.VMEM((SUBLANES, 2 * D_FF), F32),
            pltpu.VMEM((tm, D_FF), BF16),
        ],
        compiler_params=pltpu.CompilerParams(
            dimension_semantics=("arbitrary",), vmem_limit_bytes=VMEM_LIMIT_BYTES),
        name="ffn",
    )(x1, h2, w_up.astype(BF16), fw, ffn_conv_b.reshape(1, 2 * D_FF), w_down.astype(BF16))


def kernel(x, norm1_g, w_in, gate_b, conv_w, conv_b, conv_norm_g, w_conv_out, q_norm_g, k_norm_g,
           w_attn_out, w_out, norm2_g, w_up, ffn_conv_w, ffn_conv_b, w_down):
    batch, seq, d = x.shape
    assert d == D_MODEL and seq % TM_IN == 0 and seq % TM_OUT == 0 and seq % TM_FFN == 0
    assert all(seq % (dil * ATTN_BLOCK) == 0 for _, dil in DILATION_GROUPS)
    depth = norm1_g.shape[0]
    x2d = x.reshape(batch * seq, d)
    for l in range(depth):
        a, q, k, v, g = _in_proj(x2d, norm1_g[l], w_in[l], gate_b[l], q_norm_g[l], k_norm_g[l])
        o = _attention(q, k, v, batch, seq)
        x1, h2 = _token_out(x2d, a, o, g, conv_w[l], conv_b[l], conv_norm_g[l], w_conv_out[l],
                            w_attn_out[l], w_out[l], norm2_g[l], seq)
        x2d = _ffn(x1, h2, w_up[l], ffn_conv_w[l], ffn_conv_b[l], w_down[l], seq)
    return x2d.reshape(batch, seq, d)
```

```python
import functools

import jax
import jax.numpy as jnp
from jax import lax
from jax.experimental import pallas as pl
from jax.experimental.pallas import tpu as pltpu

F32 = jnp.float32
BF16 = jnp.bfloat16

D_MODEL = 1024
CONV_CH = D_MODEL
CONV_WIDTH = 31
N_HEADS = 16
HEAD_DIM = 64
ATTN_DIM = N_HEADS * HEAD_DIM
DILATION_GROUPS = ((128, 1), (512, 4), (2048, 16))
ATTN_BLOCK = 128
D_FF = 2816
FFN_CONV_WIDTH = 3
EPS = 1e-6

LANES = 128
SUBLANES = 8
VMEM_LIMIT_BYTES = 56 * 1024 * 1024
MASK_STEPS = 1e30
LOG2E = 1.4426950408889634

TM_IN = 512
TM_OUT = 256
TM_FFN = 512
HALO = 32
CONV_BF16_TAPS = 4
NORM_CHUNK = 256
FFN_CHUNK = 256
ATTN_LOOKAHEAD = 3
SPLIT = 4


def _dot(a, b):
    return jnp.dot(a, b, preferred_element_type=F32)


def _sigmoid(x):
    return 1.0 / (1.0 + jnp.exp(-x))


def _resident(shape):
    return pl.BlockSpec(shape, lambda *_: (0,) * len(shape), pipeline_mode=pl.Buffered(1))


def _in_proj_kernel(x_ref, g1_ref, w_ref, gb_ref, qg_ref, kg_ref, pm_ref,
                    a_ref, q_ref, k_ref, v_ref, g_ref):
    x = x_ref[...]
    ms = jnp.mean(x * x, axis=-1, keepdims=True)
    h = (x * lax.rsqrt(ms + EPS) * g1_ref[...]).astype(BF16)

    tm = x.shape[0]
    c2 = 2 * CONV_CH
    n_chunks = ATTN_DIM // NORM_CHUNK

    def head_msq(z):
        sq = (z * z).astype(BF16)
        stacked = jnp.concatenate([sq[:, c * NORM_CHUNK:(c + 1) * NORM_CHUNK] for c in range(n_chunks)], axis=0)
        msq = _dot(stacked, pm_ref[...])
        return jnp.concatenate([msq[c * tm:(c + 1) * tm, :] for c in range(n_chunks)], axis=1)

    zglu = _dot(h, w_ref[:, 0:c2])
    zq = _dot(h, w_ref[:, c2:c2 + ATTN_DIM])
    a_ref[...] = (zglu[:, 0:CONV_CH] * _sigmoid(zglu[:, CONV_CH:c2])).astype(BF16)
    zk = _dot(h, w_ref[:, c2 + ATTN_DIM:c2 + 2 * ATTN_DIM])
    msq_q = head_msq(zq)
    q_ref[...] = (zq * lax.rsqrt(msq_q + EPS) * qg_ref[...] * (HEAD_DIM ** -0.5 * LOG2E)).astype(BF16)
    zvg = _dot(h, w_ref[:, c2 + 2 * ATTN_DIM:])
    msq_k = head_msq(zk)
    k_ref[...] = (zk * lax.rsqrt(msq_k + EPS) * kg_ref[...]).astype(BF16)
    v_ref[...] = zvg[:, 0:ATTN_DIM].astype(BF16)
    g_ref[...] = _sigmoid(zvg[:, ATTN_DIM:] + gb_ref[...]).astype(BF16)


def _in_proj(x2d, norm1_g, w_in, gate_b, q_norm_g, k_norm_g):
    t = x2d.shape[0]
    in_cols = w_in.shape[1]
    idx = jnp.arange(NORM_CHUNK) // HEAD_DIM
    pmat = jnp.where(idx[:, None] == idx[None, :], 1.0 / HEAD_DIM, 0.0).astype(BF16)
    qg = jnp.tile(q_norm_g.astype(F32), N_HEADS).reshape(1, ATTN_DIM)
    kg = jnp.tile(k_norm_g.astype(F32), N_HEADS).reshape(1, ATTN_DIM)
    row = lambda i: (i, 0)
    tok = lambda width, dt: jax.ShapeDtypeStruct((t, width), dt)
    return pl.pallas_call(
        _in_proj_kernel,
        grid=(t // TM_IN,),
        in_specs=[
            pl.BlockSpec((TM_IN, D_MODEL), row),
            _resident((1, D_MODEL)),
            _resident((D_MODEL, in_cols)),
            _resident((1, 2 * D_MODEL)),
            _resident((1, ATTN_DIM)),
            _resident((1, ATTN_DIM)),
            _resident((NORM_CHUNK, NORM_CHUNK)),
        ],
        out_specs=[
            pl.BlockSpec((TM_IN, CONV_CH), row),
            pl.BlockSpec((TM_IN, ATTN_DIM), row),
            pl.BlockSpec((TM_IN, ATTN_DIM), row),
            pl.BlockSpec((TM_IN, ATTN_DIM), row),
            pl.BlockSpec((TM_IN, 2 * D_MODEL), row),
        ],
        out_shape=[tok(CONV_CH, BF16), tok(ATTN_DIM, BF16), tok(ATTN_DIM, BF16),
                   tok(ATTN_DIM, BF16), tok(2 * D_MODEL, BF16)],
        compiler_params=pltpu.CompilerParams(
            dimension_semantics=("parallel",), vmem_limit_bytes=VMEM_LIMIT_BYTES),
        name="in_proj",
    )(x2d, norm1_g.reshape(1, D_MODEL), w_in.astype(BF16), gate_b.reshape(1, 2 * D_MODEL), qg, kg, pmat)


def _attn_kernel(slopes_ref, q_ref, k_ref, v_ref, o_ref, sta, stb, qb, kb, vb, bias, og, lg, *, seq):
    blk = ATTN_BLOCK
    ngroups = len(DILATION_GROUPS)
    dils = [d for _, d in DILATION_GROUPS]
    assert dils[0] == 1 and all(dils[g] == SPLIT * dils[g - 1] for g in range(1, ngroups))
    hp = pl.program_id(1)

    def kv_slot_rows(g):
        sub_len = seq // dils[g]
        return sub_len + (blk if sub_len > blk else 0)

    lane_q = lax.broadcasted_iota(jnp.int32, (blk, LANES), 1)
    first_q = lane_q < HEAD_DIM

    def put(kind, g, row0, val):
        n = val.shape[0]
        if kind == "q":
            qb[g, row0:row0 + n, :] = val.astype(BF16)
        elif kind == "k":
            kb[g, row0:row0 + n, :] = val.astype(BF16)
        else:
            first_v = lax.broadcasted_iota(jnp.int32, val.shape, 1) < HEAD_DIM
            vb[g, 0, row0:row0 + n, :] = jnp.where(first_v, val, 1.0).astype(BF16)
            vb[g, 1, row0:row0 + n, :] = jnp.where(first_v, 1.0, val).astype(BF16)

    for t_ref, kind in ((q_ref, "q"), (k_ref, "k"), (v_ref, "v")):
        cur, nxt = sta, stb
        for g in range(ngroups):
            sub_len = seq // dils[g]
            slot = sub_len if kind == "q" else kv_slot_rows(g)
            front = slot - sub_len
            for j in range(dils[g]):
                if front:
                    put(kind, g, j * slot, jnp.zeros((front, LANES), F32))
            if g == 0:
                val = t_ref[...].astype(F32)
                cur[...] = val
                put(kind, g, front, val)
                continue
            prev_len = seq // dils[g - 1]
            for jp in range(dils[g - 1]):
                for r in range(SPLIT):
                    j = jp * SPLIT + r
                    val = cur[pl.ds(jp * prev_len + r, sub_len, stride=SPLIT), :]
                    if g + 1 < ngroups:
                        nxt[j * sub_len:(j + 1) * sub_len, :] = val
                    put(kind, g, j * slot + front, val)
            cur, nxt = nxt, cur

    rows = lax.broadcasted_iota(jnp.int32, (blk, 2 * blk), 0)
    cols = lax.broadcasted_iota(jnp.int32, (blk, 2 * blk), 1)
    steps2 = rows + blk - cols
    rows1 = lax.broadcasted_iota(jnp.int32, (blk, blk), 0)
    cols1 = lax.broadcasted_iota(jnp.int32, (blk, blk), 1)
    steps1 = rows1 - cols1

    def scores(qt, kt, bias_of_head):
        out = []
        for hh in range(2):
            sel_q = first_q if hh == 0 else ~first_q
            qh = jnp.where(sel_q, qt, jnp.zeros_like(qt))
            s = lax.dot_general(qh, kt, (((1,), (1,)), ((), ())), preferred_element_type=F32)
            out.append(s + bias_of_head(hh))
        return out

    def weighted_values(ss, v_of_head):
        res, mx = [], []
        for hh in range(2):
            m = jnp.max(ss[hh], axis=-1, keepdims=True)
            p = jnp.exp2(ss[hh] - m).astype(BF16)
            res.append(_dot(p, v_of_head(hh)))
            mx.append(m)
        acc = jnp.where(first_q, res[0], res[1])
        den = pltpu.roll(jnp.where(first_q, res[1], res[0]), HEAD_DIM, 1)
        mm = jnp.where(first_q, mx[0], mx[1])
        return acc / den, mm + jnp.log2(den)

    for g, (window, dil) in enumerate(DILATION_GROUPS):
        n_back = window // dil
        sub_len = seq // dil
        nb = sub_len // blk
        slot = kv_slot_rows(g)
        prev_len = seq // dils[g - 1] if g else seq
        if nb == 1:
            dist_a = jnp.where((steps1 >= 0) & (steps1 <= n_back), steps1.astype(F32), MASK_STEPS)
            variants = (dist_a,)
            nk = blk
        else:
            valid = (steps2 >= 0) & (steps2 <= n_back)
            dist_a = jnp.where(valid, steps2.astype(F32), MASK_STEPS)
            dist_b = jnp.where(valid & (cols >= blk), steps2.astype(F32), MASK_STEPS)
            variants = (dist_a, dist_b)
            nk = 2 * blk
        for var, dist in enumerate(variants):
            for hh in range(2):
                bias[2 * var + hh, :, 0:nk] = (slopes_ref[2 * hp + hh] * (-LOG2E * dil)) * dist

        n_blocks = dil * nb
        pending = {}
        for step in range(n_blocks + ATTN_LOOKAHEAD):
            if step < n_blocks:
                j, jb = divmod(step, nb)
                var = 1 if (nb > 1 and jb == 0) else 0
                pending[step] = scores(
                    qb[g, step * blk:(step + 1) * blk, :], kb[g, j * slot + jb * blk:j * slot + jb * blk + nk, :],
                    lambda hh, var=var, nk=nk: bias[2 * var + hh, :, 0:nk])
            u = step - ATTN_LOOKAHEAD
            if u >= 0:
                j, jb = divmod(u, nb)
                k0 = j * slot + jb * blk
                o, l = weighted_values(pending.pop(u), lambda hh, g=g, k0=k0, nk=nk: vb[g, hh, k0:k0 + nk, :])
                if g == 0:
                    dst = pl.ds(u * blk, blk)
                else:
                    jp, r = divmod(j, SPLIT)
                    dst = pl.ds(jp * prev_len + r + SPLIT * jb * blk, blk, stride=SPLIT)
                og[g, dst, :] = o
                lg[g, dst, :] = l

    for res in (og, lg):
        for g in range(2, ngroups):
            for gg in range(g - 1, 0, -1):
                sub_len = seq // dils[gg]
                prev_len = seq // dils[gg - 1]
                sta[...] = res[g]
                for j in range(dils[gg - 1]):
                    for r in range(SPLIT):
                        src0 = (j * SPLIT + r) * sub_len
                        res[g, pl.ds(j * prev_len + r, sub_len, stride=SPLIT), :] = sta[src0:src0 + sub_len, :]

    for i in range(seq // blk):
        r0 = i * blk
        ls = [lg[g, pl.ds(r0, blk), :] for g in range(ngroups)]
        mmax = functools.reduce(jnp.maximum, ls)
        ws = [jnp.exp2(l - mmax) for l in ls]
        wsum = functools.reduce(lambda a, b: a + b, ws)
        acc = ws[0] * og[0, pl.ds(r0, blk), :]
        for g in range(1, ngroups):
            acc = acc + ws[g] * og[g, pl.ds(r0, blk), :]
        o_ref[pl.ds(r0, blk), :] = (acc / wsum).astype(o_ref.dtype)


def _attention(q, k, v, batch, seq):
    slopes = 2.0 ** (-8.0 * jnp.arange(1, N_HEADS + 1, dtype=F32) / N_HEADS)
    q3, k3, v3 = (t.reshape(batch, seq, ATTN_DIM) for t in (q, k, v))
    ngroups = len(DILATION_GROUPS)
    kv_rows = max(seq + d * (ATTN_BLOCK if seq // d > ATTN_BLOCK else 0) for _, d in DILATION_GROUPS)
    blk = pl.BlockSpec((None, seq, LANES), lambda b, h: (b, 0, h))
    out = pl.pallas_call(
        functools.partial(_attn_kernel, seq=seq),
        grid=(batch, ATTN_DIM // LANES),
        in_specs=[pl.BlockSpec(memory_space=pltpu.SMEM), blk, blk, blk],
        out_specs=blk,
        out_shape=jax.ShapeDtypeStruct((batch, seq, ATTN_DIM), BF16),
        scratch_shapes=[
            pltpu.VMEM((seq, LANES), F32),
            pltpu.VMEM((seq, LANES), F32),
            pltpu.VMEM((ngroups, seq, LANES), BF16),
            pltpu.VMEM((ngroups, kv_rows, LANES), BF16),
            pltpu.VMEM((ngroups, 2, kv_rows, LANES), BF16),
            pltpu.VMEM((4, ATTN_BLOCK, 2 * ATTN_BLOCK), F32),
            pltpu.VMEM((ngroups, seq, LANES), F32),
            pltpu.VMEM((ngroups, seq, LANES), F32),
        ],
        compiler_params=pltpu.CompilerParams(
            dimension_semantics=("parallel", "parallel"), vmem_limit_bytes=VMEM_LIMIT_BYTES),
        name="attention",
    )(slopes, q3, k3, v3)
    return out.reshape(batch * seq, ATTN_DIM)


def _token_out_kernel(x_ref, a_ref, halo_ref, o_ref, g_ref, cw_ref, cb_ref, cg_ref,
                      wc_ref, wa_ref, wo_ref, g2_ref, x1_ref, h2_ref, abuf, cbuf, ybuf, *, tiles_per_seq):
    tm = TM_OUT
    first = (pl.program_id(0) % tiles_per_seq) == 0
    n_slabs = CONV_CH // LANES
    half = tm // 2
    u32 = jnp.uint32
    top16 = u32(0xFFFF0000)
    for c in range(n_slabs):
        lanes = slice(c * LANES, (c + 1) * LANES)
        halo = jnp.where(first, 0.0, halo_ref[:, lanes].astype(F32))
        bits = pltpu.bitcast(jnp.concatenate([halo, a_ref[:, lanes].astype(F32)], axis=0), u32)
        low = lax.shift_right_logical(bits[0:HALO + half, :], u32(16))
        abuf[c, pl.ds(0, HALO + half, stride=2), :] = low | (bits[half:HALO + tm, :] & top16)

    off = HALO - (CONV_WIDTH - 1)
    ybuf[...] = _dot(o_ref[...], wa_ref[...])

    def conv_slab(c, carry):
        lanes = pl.ds(pl.multiple_of(c * LANES, LANES), LANES)
        acc_low = jnp.zeros((half, LANES), F32) + cb_ref[:, lanes]
        acc_high = acc_low
        for j0 in range(0, CONV_WIDTH, CONV_BF16_TAPS):
            group = None
            for j in range(j0, min(j0 + CONV_BF16_TAPS, CONV_WIDTH)):
                window = pltpu.bitcast(abuf[c, pl.ds(2 * (off + j), half, stride=2), :], BF16)
                term = cw_ref[pl.ds(j, 1), lanes].astype(BF16) * window
                group = term if group is None else group + term
            gbits = pltpu.bitcast(group, u32)
            acc_low = acc_low + pltpu.bitcast(lax.shift_left(gbits, u32(16)), F32)
            acc_high = acc_high + pltpu.bitcast(gbits & top16, F32)
        cbuf[pl.ds(0, half), lanes] = acc_low
        cbuf[pl.ds(half, half), lanes] = acc_high
        return carry

    lax.fori_loop(0, n_slabs, conv_slab, 0)

    cv = cbuf[...]
    ms = jnp.mean(cv * cv, axis=-1, keepdims=True)
    cn = cv * lax.rsqrt(ms + EPS) * cg_ref[...]
    ac = (cn * _sigmoid(cn)).astype(BF16)

    ya = _dot(ac, wc_ref[...])
    yb = ybuf[...]
    ga = g_ref[:, 0:D_MODEL].astype(F32)
    gb = g_ref[:, D_MODEL:2 * D_MODEL].astype(F32)
    mix = (ga * ya + gb * yb).astype(BF16)
    x1 = x_ref[...] + _dot(mix, wo_ref[...])
    x1_ref[...] = x1
    ms2 = jnp.mean(x1 * x1, axis=-1, keepdims=True)
    h2_ref[...] = (x1 * lax.rsqrt(ms2 + EPS) * g2_ref[...]).astype(BF16)


def _token_out(x2d, a, o, g, conv_w, conv_b, conv_norm_g, w_conv_out, w_attn_out, w_out, norm2_g, seq):
    t = x2d.shape[0]
    tm = TM_OUT
    tiles_per_seq = seq // tm
    halo_per_tile = tm // HALO
    cw = jnp.zeros((HALO, CONV_CH), F32).at[:CONV_WIDTH].set(conv_w.astype(F32))
    row = lambda i: (i, 0)
    return pl.pallas_call(
        functools.partial(_token_out_kernel, tiles_per_seq=tiles_per_seq),
        grid=(t // tm,),
        in_specs=[
            pl.BlockSpec((tm, D_MODEL), row),
            pl.BlockSpec((tm, CONV_CH), row),
            pl.BlockSpec((HALO, CONV_CH), lambda i: (jnp.maximum(i * halo_per_tile - 1, 0), 0)),
            pl.BlockSpec((tm, ATTN_DIM), row),
            pl.BlockSpec((tm, 2 * D_MODEL), row),
            _resident((HALO, CONV_CH)),
            _resident((1, CONV_CH)),
            _resident((1, CONV_CH)),
            _resident((CONV_CH, D_MODEL)),
            _resident((ATTN_DIM, D_MODEL)),
            _resident((D_MODEL, D_MODEL)),
            _resident((1, D_MODEL)),
        ],
        out_specs=[pl.BlockSpec((tm, D_MODEL), row), pl.BlockSpec((tm, D_MODEL), row)],
        out_shape=[jax.ShapeDtypeStruct((t, D_MODEL), F32), jax.ShapeDtypeStruct((t, D_MODEL), BF16)],
        scratch_shapes=[pltpu.VMEM((CONV_CH // LANES, 2 * (HALO + tm // 2), LANES), jnp.uint32),
                        pltpu.VMEM((tm, CONV_CH), F32), pltpu.VMEM((tm, D_MODEL), F32)],
        compiler_params=pltpu.CompilerParams(
            dimension_semantics=("parallel",), vmem_limit_bytes=VMEM_LIMIT_BYTES),
        name="token_out",
    )(x2d, a, a, o, g, cw, conv_b.reshape(1, CONV_CH), conv_norm_g.reshape(1, CONV_CH),
      w_conv_out.astype(BF16), w_attn_out.astype(BF16), w_out.astype(BF16), norm2_g.reshape(1, D_MODEL))


def _ffn_kernel(x1_ref, h2_ref, wup_ref, fw_ref, fb_ref, wdn_ref, x2_ref, ubuf, carry, act, *, tiles_per_seq):
    tm = TM_FFN
    first = (pl.program_id(0) % tiles_per_seq) == 0

    @pl.when(first)
    def _():
        carry[...] = jnp.zeros_like(carry)

    h2 = h2_ref[...]
    top = SUBLANES
    taps = FFN_CONV_WIDTH

    def conv(col0):
        u = _dot(h2, wup_ref[:, col0:col0 + FFN_CHUNK])
        ubuf[0:top, :] = carry[:, col0:col0 + FFN_CHUNK]
        ubuf[top:top + tm, :] = u
        carry[:, col0:col0 + FFN_CHUNK] = u[tm - top:tm, :]
        out = fb_ref[:, col0:col0 + FFN_CHUNK] + fw_ref[taps - 1:taps, col0:col0 + FFN_CHUNK] * u
        for j in range(taps - 1):
            shift = taps - 1 - j
            out = out + fw_ref[j:j + 1, col0:col0 + FFN_CHUNK] * ubuf[top - shift:top - shift + tm, :]
        return out

    for c in range(0, D_FF, FFN_CHUNK):
        uv = conv(c)
        ug = conv(D_FF + c)
        act[:, c:c + FFN_CHUNK] = (ug * _sigmoid(ug) * uv).astype(BF16)

    x2_ref[...] = x1_ref[...] + _dot(act[...], wdn_ref[...])


def _ffn(x1, h2, w_up, ffn_conv_w, ffn_conv_b, w_down, seq):
    t = x1.shape[0]
    tm = TM_FFN
    fw = jnp.zeros((SUBLANES, 2 * D_FF), F32).at[:FFN_CONV_WIDTH].set(ffn_conv_w.astype(F32))
    row = lambda i: (i, 0)
    return pl.pallas_call(
        functools.partial(_ffn_kernel, tiles_per_seq=seq // tm),
        grid=(t // tm,),
        in_specs=[
            pl.BlockSpec((tm, D_MODEL), row),
            pl.BlockSpec((tm, D_MODEL), row),
            _resident((D_MODEL, 2 * D_FF)),
            _resident((SUBLANES, 2 * D_FF)),
            _resident((1, 2 * D_FF)),
            _resident((D_FF, D_MODEL)),
        ],
        out_specs=pl.BlockSpec((tm, D_MODEL), row),
        out_shape=jax.ShapeDtypeStruct((t, D_MODEL), F32),
        scratch_shapes=[
            pltpu.VMEM((SUBLANES + tm, FFN_CHUNK), F32),
            pltpu.VMEM((SUBLANES, 2 * D_FF), F32),
            pltpu.VMEM((tm, D_FF), BF16),
        ],
        compiler_params=pltpu.CompilerParams(
            dimension_semantics=("arbitrary",), vmem_limit_bytes=VMEM_LIMIT_BYTES),
        name="ffn",
    )(x1, h2, w_up.astype(BF16), fw, ffn_conv_b.reshape(1, 2 * D_FF), w_down.astype(BF16))


def kernel(x, norm1_g, w_in, gate_b, conv_w, conv_b, conv_norm_g, w_conv_out, q_norm_g, k_norm_g,
           w_attn_out, w_out, norm2_g, w_up, ffn_conv_w, ffn_conv_b, w_down):
    batch, seq, d = x.shape
    assert d == D_MODEL and seq % TM_IN == 0 and seq % TM_OUT == 0 and seq % TM_FFN == 0
    assert all(seq % (dil * ATTN_BLOCK) == 0 for _, dil in DILATION_GROUPS)
    depth = norm1_g.shape[0]
    x2d = x.reshape(batch * seq, d)
    for l in range(depth):
        a, q, k, v, g = _in_proj(x2d, norm1_g[l], w_in[l], gate_b[l], q_norm_g[l], k_norm_g[l])
        o = _attention(q, k, v, batch, seq)
        x1, h2 = _token_out(x2d, a, o, g, conv_w[l], conv_b[l], conv_norm_g[l], w_conv_out[l],
                            w_attn_out[l], w_out[l], norm2_g[l], seq)
        x2d = _ffn(x1, h2, w_up[l], ffn_conv_w[l], ffn_conv_b[l], w_down[l], seq)
    return x2d.reshape(batch, seq, d)
```

```python
import functools

import jax
import jax.numpy as jnp
from jax import lax
from jax.experimental import pallas as pl
from jax.experimental.pallas import tpu as pltpu

F32 = jnp.float32
BF16 = jnp.bfloat16

D_MODEL = 1024
CONV_CH = D_MODEL
CONV_WIDTH = 31
N_HEADS = 16
HEAD_DIM = 64
ATTN_DIM = N_HEADS * HEAD_DIM
DILATION_GROUPS = ((128, 1), (512, 4), (2048, 16))
ATTN_BLOCK = 128
D_FF = 2816
FFN_CONV_WIDTH = 3
EPS = 1e-6

LANES = 128
SUBLANES = 8
VMEM_LIMIT_BYTES = 56 * 1024 * 1024
MASK_STEPS = 1e30
LOG2E = 1.4426950408889634

TM_IN = 512
TM_OUT = 512
TM_FFN = 512
HALO = 32
CONV_BF16_TAPS = 4
NORM_CHUNK = 256
FFN_CHUNK = 256
ATTN_LOOKAHEAD = 3
SPLIT = 4


def _dot(a, b):
    return jnp.dot(a, b, preferred_element_type=F32)


def _sigmoid(x):
    return 1.0 / (1.0 + jnp.exp(-x))


def _resident(shape):
    return pl.BlockSpec(shape, lambda *_: (0,) * len(shape), pipeline_mode=pl.Buffered(1))


def _in_proj_kernel(x_ref, g1_ref, w_ref, gb_ref, qg_ref, kg_ref, pm_ref,
                    a_ref, q_ref, k_ref, v_ref, g_ref):
    x = x_ref[...]
    ms = jnp.mean(x * x, axis=-1, keepdims=True)
    h = (x * lax.rsqrt(ms + EPS) * g1_ref[...]).astype(BF16)

    tm = x.shape[0]
    c2 = 2 * CONV_CH
    n_chunks = ATTN_DIM // NORM_CHUNK

    def head_msq(z):
        sq = (z * z).astype(BF16)
        stacked = jnp.concatenate([sq[:, c * NORM_CHUNK:(c + 1) * NORM_CHUNK] for c in range(n_chunks)], axis=0)
        msq = _dot(stacked, pm_ref[...])
        return jnp.concatenate([msq[c * tm:(c + 1) * tm, :] for c in range(n_chunks)], axis=1)

    zglu = _dot(h, w_ref[:, 0:c2])
    zq = _dot(h, w_ref[:, c2:c2 + ATTN_DIM])
    a_ref[...] = (zglu[:, 0:CONV_CH] * _sigmoid(zglu[:, CONV_CH:c2])).astype(BF16)
    zk = _dot(h, w_ref[:, c2 + ATTN_DIM:c2 + 2 * ATTN_DIM])
    msq_q = head_msq(zq)
    q_ref[...] = (zq * lax.rsqrt(msq_q + EPS) * qg_ref[...] * (HEAD_DIM ** -0.5 * LOG2E)).astype(BF16)
    zvg = _dot(h, w_ref[:, c2 + 2 * ATTN_DIM:])
    msq_k = head_msq(zk)
    k_ref[...] = (zk * lax.rsqrt(msq_k + EPS) * kg_ref[...]).astype(BF16)
    v_ref[...] = zvg[:, 0:ATTN_DIM].astype(BF16)
    g_ref[...] = _sigmoid(zvg[:, ATTN_DIM:] + gb_ref[...]).astype(BF16)


def _in_proj(x2d, norm1_g, w_in, gate_b, q_norm_g, k_norm_g):
    t = x2d.shape[0]
    in_cols = w_in.shape[1]
    idx = jnp.arange(NORM_CHUNK) // HEAD_DIM
    pmat = jnp.where(idx[:, None] == idx[None, :], 1.0 / HEAD_DIM, 0.0).astype(BF16)
    qg = jnp.tile(q_norm_g.astype(F32), N_HEADS).reshape(1, ATTN_DIM)
    kg = jnp.tile(k_norm_g.astype(F32), N_HEADS).reshape(1, ATTN_DIM)
    row = lambda i: (i, 0)
    tok = lambda width, dt: jax.ShapeDtypeStruct((t, width), dt)
    return pl.pallas_call(
        _in_proj_kernel,
        grid=(t // TM_IN,),
        in_specs=[
            pl.BlockSpec((TM_IN, D_MODEL), row),
            _resident((1, D_MODEL)),
            _resident((D_MODEL, in_cols)),
            _resident((1, 2 * D_MODEL)),
            _resident((1, ATTN_DIM)),
            _resident((1, ATTN_DIM)),
            _resident((NORM_CHUNK, NORM_CHUNK)),
        ],
        out_specs=[
            pl.BlockSpec((TM_IN, CONV_CH), row),
            pl.BlockSpec((TM_IN, ATTN_DIM), row),
            pl.BlockSpec((TM_IN, ATTN_DIM), row),
            pl.BlockSpec((TM_IN, ATTN_DIM), row),
            pl.BlockSpec((TM_IN, 2 * D_MODEL), row),
        ],
        out_shape=[tok(CONV_CH, BF16), tok(ATTN_DIM, BF16), tok(ATTN_DIM, BF16),
                   tok(ATTN_DIM, BF16), tok(2 * D_MODEL, BF16)],
        compiler_params=pltpu.CompilerParams(
            dimension_semantics=("parallel",), vmem_limit_bytes=VMEM_LIMIT_BYTES),
        name="in_proj",
    )(x2d, norm1_g.reshape(1, D_MODEL), w_in.astype(BF16), gate_b.reshape(1, 2 * D_MODEL), qg, kg, pmat)


def _attn_kernel(slopes_ref, q_ref, k_ref, v_ref, o_ref, sta, stb, qb, kb, vb, bias, og, lg, *, seq):
    blk = ATTN_BLOCK
    ngroups = len(DILATION_GROUPS)
    dils = [d for _, d in DILATION_GROUPS]
    assert dils[0] == 1 and all(dils[g] == SPLIT * dils[g - 1] for g in range(1, ngroups))
    hp = pl.program_id(1)

    def kv_slot_rows(g):
        sub_len = seq // dils[g]
        return sub_len + (blk if sub_len > blk else 0)

    lane_q = lax.broadcasted_iota(jnp.int32, (blk, LANES), 1)
    first_q = lane_q < HEAD_DIM

    def put(kind, g, row0, val):
        n = val.shape[0]
        if kind == "q":
            qb[g, row0:row0 + n, :] = val.astype(BF16)
        elif kind == "k":
            kb[g, row0:row0 + n, :] = val.astype(BF16)
        else:
            first_v = lax.broadcasted_iota(jnp.int32, val.shape, 1) < HEAD_DIM
            vb[g, 0, row0:row0 + n, :] = jnp.where(first_v, val, 1.0).astype(BF16)
            vb[g, 1, row0:row0 + n, :] = jnp.where(first_v, 1.0, val).astype(BF16)

    for t_ref, kind in ((q_ref, "q"), (k_ref, "k"), (v_ref, "v")):
        cur, nxt = sta, stb
        for g in range(ngroups):
            sub_len = seq // dils[g]
            slot = sub_len if kind == "q" else kv_slot_rows(g)
            front = slot - sub_len
            for j in range(dils[g]):
                if front:
                    put(kind, g, j * slot, jnp.zeros((front, LANES), F32))
            if g == 0:
                val = t_ref[...].astype(F32)
                cur[...] = val
                put(kind, g, front, val)
                continue
            prev_len = seq // dils[g - 1]
            for jp in range(dils[g - 1]):
                for r in range(SPLIT):
                    j = jp * SPLIT + r
                    val = cur[pl.ds(jp * prev_len + r, sub_len, stride=SPLIT), :]
                    if g + 1 < ngroups:
                        nxt[j * sub_len:(j + 1) * sub_len, :] = val
                    put(kind, g, j * slot + front, val)
            cur, nxt = nxt, cur

    rows = lax.broadcasted_iota(jnp.int32, (blk, 2 * blk), 0)
    cols = lax.broadcasted_iota(jnp.int32, (blk, 2 * blk), 1)
    steps2 = rows + blk - cols
    rows1 = lax.broadcasted_iota(jnp.int32, (blk, blk), 0)
    cols1 = lax.broadcasted_iota(jnp.int32, (blk, blk), 1)
    steps1 = rows1 - cols1

    def scores(qt, kt, bias_of_head):
        out = []
        for hh in range(2):
            sel_q = first_q if hh == 0 else ~first_q
            qh = jnp.where(sel_q, qt, jnp.zeros_like(qt))
            s = lax.dot_general(qh, kt, (((1,), (1,)), ((), ())), preferred_element_type=F32)
            out.append(s + bias_of_head(hh))
        return out

    def weighted_values(ss, v_of_head):
        res, mx = [], []
        for hh in range(2):
            m = jnp.max(ss[hh], axis=-1, keepdims=True)
            p = jnp.exp2(ss[hh] - m).astype(BF16)
            res.append(_dot(p, v_of_head(hh)))
            mx.append(m)
        acc = jnp.where(first_q, res[0], res[1])
        den = pltpu.roll(jnp.where(first_q, res[1], res[0]), HEAD_DIM, 1)
        mm = jnp.where(first_q, mx[0], mx[1])
        return acc / den, mm + jnp.log2(den)

    for g, (window, dil) in enumerate(DILATION_GROUPS):
        n_back = window // dil
        sub_len = seq // dil
        nb = sub_len // blk
        slot = kv_slot_rows(g)
        prev_len = seq // dils[g - 1] if g else seq
        if nb == 1:
            dist_a = jnp.where((steps1 >= 0) & (steps1 <= n_back), steps1.astype(F32), MASK_STEPS)
            variants = (dist_a,)
            nk = blk
        else:
            valid = (steps2 >= 0) & (steps2 <= n_back)
            dist_a = jnp.where(valid, steps2.astype(F32), MASK_STEPS)
            dist_b = jnp.where(valid & (cols >= blk), steps2.astype(F32), MASK_STEPS)
            variants = (dist_a, dist_b)
            nk = 2 * blk
        for var, dist in enumerate(variants):
            for hh in range(2):
                bias[2 * var + hh, :, 0:nk] = (slopes_ref[2 * hp + hh] * (-LOG2E * dil)) * dist

        n_blocks = dil * nb
        pending = {}
        for step in range(n_blocks + ATTN_LOOKAHEAD):
            if step < n_blocks:
                j, jb = divmod(step, nb)
                var = 1 if (nb > 1 and jb == 0) else 0
                pending[step] = scores(
                    qb[g, step * blk:(step + 1) * blk, :], kb[g, j * slot + jb * blk:j * slot + jb * blk + nk, :],
                    lambda hh, var=var, nk=nk: bias[2 * var + hh, :, 0:nk])
            u = step - ATTN_LOOKAHEAD
            if u >= 0:
                j, jb = divmod(u, nb)
                k0 = j * slot + jb * blk
                o, l = weighted_values(pending.pop(u), lambda hh, g=g, k0=k0, nk=nk: vb[g, hh, k0:k0 + nk, :])
                if g == 0:
                    dst = pl.ds(u * blk, blk)
                else:
                    jp, r = divmod(j, SPLIT)
                    dst = pl.ds(jp * prev_len + r + SPLIT * jb * blk, blk, stride=SPLIT)
                og[g, dst, :] = o
                lg[g, dst, :] = l

    for res in (og, lg):
        for g in range(2, ngroups):
            for gg in range(g - 1, 0, -1):
                sub_len = seq // dils[gg]
                prev_len = seq // dils[gg - 1]
                sta[...] = res[g]
                for j in range(dils[gg - 1]):
                    for r in range(SPLIT):
                        src0 = (j * SPLIT + r) * sub_len
                        res[g, pl.ds(j * prev_len + r, sub_len, stride=SPLIT), :] = sta[src0:src0 + sub_len, :]

    for i in range(seq // blk):
        r0 = i * blk
        ls = [lg[g, pl.ds(r0, blk), :] for g in range(ngroups)]
        mmax = functools.reduce(jnp.maximum, ls)
        ws = [jnp.exp2(l - mmax) for l in ls]
        wsum = functools.reduce(lambda a, b: a + b, ws)
        acc = ws[0] * og[0, pl.ds(r0, blk), :]
        for g in range(1, ngroups):
            acc = acc + ws[g] * og[g, pl.ds(r0, blk), :]
        o_ref[pl.ds(r0, blk), :] = (acc / wsum).astype(o_ref.dtype)


def _attention(q, k, v, batch, seq):
    slopes = 2.0 ** (-8.0 * jnp.arange(1, N_HEADS + 1, dtype=F32) / N_HEADS)
    q3, k3, v3 = (t.reshape(batch, seq, ATTN_DIM) for t in (q, k, v))
    ngroups = len(DILATION_GROUPS)
    kv_rows = max(seq + d * (ATTN_BLOCK if seq // d > ATTN_BLOCK else 0) for _, d in DILATION_GROUPS)
    blk = pl.BlockSpec((None, seq, LANES), lambda b, h: (b, 0, h))
    out = pl.pallas_call(
        functools.partial(_attn_kernel, seq=seq),
        grid=(batch, ATTN_DIM // LANES),
        in_specs=[pl.BlockSpec(memory_space=pltpu.SMEM), blk, blk, blk],
        out_specs=blk,
        out_shape=jax.ShapeDtypeStruct((batch, seq, ATTN_DIM), BF16),
        scratch_shapes=[
            pltpu.VMEM((seq, LANES), F32),
            pltpu.VMEM((seq, LANES), F32),
            pltpu.VMEM((ngroups, seq, LANES), BF16),
            pltpu.VMEM((ngroups, kv_rows, LANES), BF16),
            pltpu.VMEM((ngroups, 2, kv_rows, LANES), BF16),
            pltpu.VMEM((4, ATTN_BLOCK, 2 * ATTN_BLOCK), F32),
            pltpu.VMEM((ngroups, seq, LANES), F32),
            pltpu.VMEM((ngroups, seq, LANES), F32),
        ],
        compiler_params=pltpu.CompilerParams(
            dimension_semantics=("parallel", "parallel"), vmem_limit_bytes=VMEM_LIMIT_BYTES),
        name="attention",
    )(slopes, q3, k3, v3)
    return out.reshape(batch * seq, ATTN_DIM)


def _token_out_kernel(x_ref, a_ref, halo_ref, o_ref, g_ref, cw_ref, cb_ref, cg_ref,
                      wc_ref, wa_ref, wo_ref, g2_ref, x1_ref, h2_ref, abuf, cbuf, *, tiles_per_seq):
    tm = TM_OUT
    first = (pl.program_id(0) % tiles_per_seq) == 0
    n_slabs = CONV_CH // LANES
    half = tm // 2
    u32 = jnp.uint32
    top16 = u32(0xFFFF0000)
    for c in range(n_slabs):
        lanes = slice(c * LANES, (c + 1) * LANES)
        halo = jnp.where(first, 0.0, halo_ref[:, lanes].astype(F32))
        bits = pltpu.bitcast(jnp.concatenate([halo, a_ref[:, lanes].astype(F32)], axis=0), u32)
        low = lax.shift_right_logical(bits[0:HALO + half, :], u32(16))
        abuf[c, pl.ds(0, HALO + half, stride=2), :] = low | (bits[half:HALO + tm, :] & top16)

    off = HALO - (CONV_WIDTH - 1)

    def conv_slab(c, carry):
        lanes = pl.ds(pl.multiple_of(c * LANES, LANES), LANES)
        acc_low = jnp.zeros((half, LANES), F32) + cb_ref[:, lanes]
        acc_high = acc_low
        for j0 in range(0, CONV_WIDTH, CONV_BF16_TAPS):
            group = None
            for j in range(j0, min(j0 + CONV_BF16_TAPS, CONV_WIDTH)):
                window = pltpu.bitcast(abuf[c, pl.ds(2 * (off + j), half, stride=2), :], BF16)
                term = cw_ref[pl.ds(j, 1), lanes].astype(BF16) * window
                group = term if group is None else group + term
            gbits = pltpu.bitcast(group, u32)
            acc_low = acc_low + pltpu.bitcast(lax.shift_left(gbits, u32(16)), F32)
            acc_high = acc_high + pltpu.bitcast(gbits & top16, F32)
        cbuf[pl.ds(0, half), lanes] = acc_low
        cbuf[pl.ds(half, half), lanes] = acc_high
        return carry

    lax.fori_loop(0, n_slabs, conv_slab, 0)

    cv = cbuf[...]
    ms = jnp.mean(cv * cv, axis=-1, keepdims=True)
    cn = cv * lax.rsqrt(ms + EPS) * cg_ref[...]
    ac = (cn * _sigmoid(cn)).astype(BF16)

    ya = _dot(ac, wc_ref[...])
    yb = _dot(o_ref[...], wa_ref[...])
    ga = g_ref[:, 0:D_MODEL].astype(F32)
    gb = g_ref[:, D_MODEL:2 * D_MODEL].astype(F32)
    mix = (ga * ya + gb * yb).astype(BF16)
    x1 = x_ref[...] + _dot(mix, wo_ref[...])
    x1_ref[...] = x1
    ms2 = jnp.mean(x1 * x1, axis=-1, keepdims=True)
    h2_ref[...] = (x1 * lax.rsqrt(ms2 + EPS) * g2_ref[...]).astype(BF16)


def _token_out(x2d, a, o, g, conv_w, conv_b, conv_norm_g, w_conv_out, w_attn_out, w_out, norm2_g, seq):
    t = x2d.shape[0]
    tm = TM_OUT
    tiles_per_seq = seq // tm
    halo_per_tile = tm // HALO
    cw = jnp.zeros((HALO, CONV_CH), F32).at[:CONV_WIDTH].set(conv_w.astype(F32))
    row = lambda i: (i, 0)
    return pl.pallas_call(
        functools.partial(_token_out_kernel, tiles_per_seq=tiles_per_seq),
        grid=(t // tm,),
        in_specs=[
            pl.BlockSpec((tm, D_MODEL), row),
            pl.BlockSpec((tm, CONV_CH), row),
            pl.BlockSpec((HALO, CONV_CH), lambda i: (jnp.maximum(i * halo_per_tile - 1, 0), 0)),
            pl.BlockSpec((tm, ATTN_DIM), row),
            pl.BlockSpec((tm, 2 * D_MODEL), row),
            _resident((HALO, CONV_CH)),
            _resident((1, CONV_CH)),
            _resident((1, CONV_CH)),
            _resident((CONV_CH, D_MODEL)),
            _resident((ATTN_DIM, D_MODEL)),
            _resident((D_MODEL, D_MODEL)),
            _resident((1, D_MODEL)),
        ],
        out_specs=[pl.BlockSpec((tm, D_MODEL), row), pl.BlockSpec((tm, D_MODEL), row)],
        out_shape=[jax.ShapeDtypeStruct((t, D_MODEL), F32), jax.ShapeDtypeStruct((t, D_MODEL), BF16)],
        scratch_shapes=[pltpu.VMEM((CONV_CH // LANES, 2 * (HALO + tm // 2), LANES), jnp.uint32),
                        pltpu.VMEM((tm, CONV_CH), F32)],
        compiler_params=pltpu.CompilerParams(
            dimension_semantics=("parallel",), vmem_limit_bytes=VMEM_LIMIT_BYTES),
        name="token_out",
    )(x2d, a, a, o, g, cw, conv_b.reshape(1, CONV_CH), conv_norm_g.reshape(1, CONV_CH),
      w_conv_out.astype(BF16), w_attn_out.astype(BF16), w_out.astype(BF16), norm2_g.reshape(1, D_MODEL))


def _ffn_kernel(x1_ref, h2_ref, wup_ref, fw_ref, fb_ref, wdn_ref, x2_ref, ubuf, carry, act, *, tiles_per_seq):
    tm = TM_FFN
    first = (pl.program_id(0) % tiles_per_seq) == 0

    @pl.when(first)
    def _():
        carry[...] = jnp.zeros_like(carry)

    h2 = h2_ref[...]
    top = SUBLANES
    taps = FFN_CONV_WIDTH

    def conv(col0):
        u = _dot(h2, wup_ref[:, col0:col0 + FFN_CHUNK])
        ubuf[0:top, :] = carry[:, col0:col0 + FFN_CHUNK]
        ubuf[top:top + tm, :] = u
        carry[:, col0:col0 + FFN_CHUNK] = u[tm - top:tm, :]
        out = fb_ref[:, col0:col0 + FFN_CHUNK] + fw_ref[taps - 1:taps, col0:col0 + FFN_CHUNK] * u
        for j in range(taps - 1):
            shift = taps - 1 - j
            out = out + fw_ref[j:j + 1, col0:col0 + FFN_CHUNK] * ubuf[top - shift:top - shift + tm, :]
        return out

    for c in range(0, D_FF, FFN_CHUNK):
        uv = conv(c)
        ug = conv(D_FF + c)
        act[:, c:c + FFN_CHUNK] = (ug * _sigmoid(ug) * uv).astype(BF16)

    x2_ref[...] = x1_ref[...] + _dot(act[...], wdn_ref[...])


def _ffn(x1, h2, w_up, ffn_conv_w, ffn_conv_b, w_down, seq):
    t = x1.shape[0]
    tm = TM_FFN
    fw = jnp.zeros((SUBLANES, 2 * D_FF), F32).at[:FFN_CONV_WIDTH].set(ffn_conv_w.astype(F32))
    row = lambda i: (i, 0)
    return pl.pallas_call(
        functools.partial(_ffn_kernel, tiles_per_seq=seq // tm),
        grid=(t // tm,),
        in_specs=[
            pl.BlockSpec((tm, D_MODEL), row),
            pl.BlockSpec((tm, D_MODEL), row),
            _resident((D_MODEL, 2 * D_FF)),
            _resident((SUBLANES, 2 * D_FF)),
            _resident((1, 2 * D_FF)),
            _resident((D_FF, D_MODEL)),
        ],
        out_specs=pl.BlockSpec((tm, D_MODEL), row),
        out_shape=jax.ShapeDtypeStruct((t, D_MODEL), F32),
        scratch_shapes=[
            pltpu.VMEM((SUBLANES + tm, FFN_CHUNK), F32),
            pltpu.VMEM((SUBLANES, 2 * D_FF), F32),
            pltpu.VMEM((tm, D_FF), BF16),
        ],
        compiler_params=pltpu.CompilerParams(
            dimension_semantics=("arbitrary",), vmem_limit_bytes=VMEM_LIMIT_BYTES),
        name="ffn",
    )(x1, h2, w_up.astype(BF16), fw, ffn_conv_b.reshape(1, 2 * D_FF), w_down.astype(BF16))


def kernel(x, norm1_g, w_in, gate_b, conv_w, conv_b, conv_norm_g, w_conv_out, q_norm_g, k_norm_g,
           w_attn_out, w_out, norm2_g, w_up, ffn_conv_w, ffn_conv_b, w_down):
    batch, seq, d = x.shape
    assert d == D_MODEL and seq % TM_IN == 0 and seq % TM_OUT == 0 and seq % TM_FFN == 0
    assert all(seq % (dil * ATTN_BLOCK) == 0 for _, dil in DILATION_GROUPS)
    depth = norm1_g.shape[0]
    x2d = x.reshape(batch * seq, d)
    for l in range(depth):
        a, q, k, v, g = _in_proj(x2d, norm1_g[l], w_in[l], gate_b[l], q_norm_g[l], k_norm_g[l])
        o = _attention(q, k, v, batch, seq)
        x1, h2 = _token_out(x2d, a, o, g, conv_w[l], conv_b[l], conv_norm_g[l], w_conv_out[l],
                            w_attn_out[l], w_out[l], norm2_g[l], seq)
        x2d = _ffn(x1, h2, w_up[l], ffn_conv_w[l], ffn_conv_b[l], w_down[l], seq)
    return x2d.reshape(batch, seq, d)
```

```python
import functools

import jax
import jax.numpy as jnp
from jax import lax
from jax.experimental import pallas as pl
from jax.experimental.pallas import tpu as pltpu

F32 = jnp.float32
BF16 = jnp.bfloat16

D_MODEL = 1024
CONV_CH = D_MODEL
CONV_WIDTH = 31
N_HEADS = 16
HEAD_DIM = 64
ATTN_DIM = N_HEADS * HEAD_DIM
DILATION_GROUPS = ((128, 1), (512, 4), (2048, 16))
ATTN_BLOCK = 128
D_FF = 2816
FFN_CONV_WIDTH = 3
EPS = 1e-6

LANES = 128
SUBLANES = 8
VMEM_LIMIT_BYTES = 56 * 1024 * 1024
MASK_STEPS = 1e30
LOG2E = 1.4426950408889634

TM_IN = 512
TM_OUT = 512
TM_FFN = 512
HALO = 32
CONV_BF16_TAPS = 4
NORM_CHUNK = 256
FFN_CHUNK = 256
ATTN_LOOKAHEAD = 3
ATTN_LOOKAHEAD_SINGLE = 8
SPLIT = 4


def _dot(a, b):
    return jnp.dot(a, b, preferred_element_type=F32)


def _sigmoid(x):
    return 1.0 / (1.0 + jnp.exp(-x))


def _resident(shape):
    return pl.BlockSpec(shape, lambda *_: (0,) * len(shape), pipeline_mode=pl.Buffered(1))


def _in_proj_kernel(x_ref, g1_ref, w_ref, gb_ref, qg_ref, kg_ref, pm_ref,
                    a_ref, q_ref, k_ref, v_ref, g_ref):
    x = x_ref[...]
    ms = jnp.mean(x * x, axis=-1, keepdims=True)
    h = (x * lax.rsqrt(ms + EPS) * g1_ref[...]).astype(BF16)

    tm = x.shape[0]
    c2 = 2 * CONV_CH
    n_chunks = ATTN_DIM // NORM_CHUNK

    def head_msq(z):
        sq = (z * z).astype(BF16)
        stacked = jnp.concatenate([sq[:, c * NORM_CHUNK:(c + 1) * NORM_CHUNK] for c in range(n_chunks)], axis=0)
        msq = _dot(stacked, pm_ref[...])
        return jnp.concatenate([msq[c * tm:(c + 1) * tm, :] for c in range(n_chunks)], axis=1)

    zglu = _dot(h, w_ref[:, 0:c2])
    zq = _dot(h, w_ref[:, c2:c2 + ATTN_DIM])
    a_ref[...] = (zglu[:, 0:CONV_CH] * _sigmoid(zglu[:, CONV_CH:c2])).astype(BF16)
    zk = _dot(h, w_ref[:, c2 + ATTN_DIM:c2 + 2 * ATTN_DIM])
    msq_q = head_msq(zq)
    q_ref[...] = (zq * lax.rsqrt(msq_q + EPS) * qg_ref[...] * (HEAD_DIM ** -0.5 * LOG2E)).astype(BF16)
    zvg = _dot(h, w_ref[:, c2 + 2 * ATTN_DIM:])
    msq_k = head_msq(zk)
    k_ref[...] = (zk * lax.rsqrt(msq_k + EPS) * kg_ref[...]).astype(BF16)
    v_ref[...] = zvg[:, 0:ATTN_DIM].astype(BF16)
    g_ref[...] = _sigmoid(zvg[:, ATTN_DIM:] + gb_ref[...]).astype(BF16)


def _in_proj(x2d, norm1_g, w_in, gate_b, q_norm_g, k_norm_g):
    t = x2d.shape[0]
    in_cols = w_in.shape[1]
    idx = jnp.arange(NORM_CHUNK) // HEAD_DIM
    pmat = jnp.where(idx[:, None] == idx[None, :], 1.0 / HEAD_DIM, 0.0).astype(BF16)
    qg = jnp.tile(q_norm_g.astype(F32), N_HEADS).reshape(1, ATTN_DIM)
    kg = jnp.tile(k_norm_g.astype(F32), N_HEADS).reshape(1, ATTN_DIM)
    row = lambda i: (i, 0)
    tok = lambda width, dt: jax.ShapeDtypeStruct((t, width), dt)
    return pl.pallas_call(
        _in_proj_kernel,
        grid=(t // TM_IN,),
        in_specs=[
            pl.BlockSpec((TM_IN, D_MODEL), row),
            _resident((1, D_MODEL)),
            _resident((D_MODEL, in_cols)),
            _resident((1, 2 * D_MODEL)),
            _resident((1, ATTN_DIM)),
            _resident((1, ATTN_DIM)),
            _resident((NORM_CHUNK, NORM_CHUNK)),
        ],
        out_specs=[
            pl.BlockSpec((TM_IN, CONV_CH), row),
            pl.BlockSpec((TM_IN, ATTN_DIM), row),
            pl.BlockSpec((TM_IN, ATTN_DIM), row),
            pl.BlockSpec((TM_IN, ATTN_DIM), row),
            pl.BlockSpec((TM_IN, 2 * D_MODEL), row),
        ],
        out_shape=[tok(CONV_CH, BF16), tok(ATTN_DIM, BF16), tok(ATTN_DIM, BF16),
                   tok(ATTN_DIM, BF16), tok(2 * D_MODEL, BF16)],
        compiler_params=pltpu.CompilerParams(
            dimension_semantics=("parallel",), vmem_limit_bytes=VMEM_LIMIT_BYTES),
        name="in_proj",
    )(x2d, norm1_g.reshape(1, D_MODEL), w_in.astype(BF16), gate_b.reshape(1, 2 * D_MODEL), qg, kg, pmat)


def _attn_kernel(slopes_ref, q_ref, k_ref, v_ref, o_ref, sta, stb, qb, kb, vb, bias, og, lg, *, seq):
    blk = ATTN_BLOCK
    ngroups = len(DILATION_GROUPS)
    dils = [d for _, d in DILATION_GROUPS]
    assert dils[0] == 1 and all(dils[g] == SPLIT * dils[g - 1] for g in range(1, ngroups))
    hp = pl.program_id(1)

    def kv_slot_rows(g):
        sub_len = seq // dils[g]
        return sub_len + (blk if sub_len > blk else 0)

    lane_q = lax.broadcasted_iota(jnp.int32, (blk, LANES), 1)
    first_q = lane_q < HEAD_DIM

    def put(kind, g, row0, val):
        n = val.shape[0]
        if kind == "q":
            qb[g, row0:row0 + n, :] = val.astype(BF16)
        elif kind == "k":
            kb[g, row0:row0 + n, :] = val.astype(BF16)
        else:
            first_v = lax.broadcasted_iota(jnp.int32, val.shape, 1) < HEAD_DIM
            vb[g, 0, row0:row0 + n, :] = jnp.where(first_v, val, 1.0).astype(BF16)
            vb[g, 1, row0:row0 + n, :] = jnp.where(first_v, 1.0, val).astype(BF16)

    for t_ref, kind in ((q_ref, "q"), (k_ref, "k"), (v_ref, "v")):
        cur, nxt = sta, stb
        for g in range(ngroups):
            sub_len = seq // dils[g]
            slot = sub_len if kind == "q" else kv_slot_rows(g)
            front = slot - sub_len
            for j in range(dils[g]):
                if front:
                    put(kind, g, j * slot, jnp.zeros((front, LANES), F32))
            if g == 0:
                val = t_ref[...].astype(F32)
                cur[...] = val
                put(kind, g, front, val)
                continue
            prev_len = seq // dils[g - 1]
            for jp in range(dils[g - 1]):
                for r in range(SPLIT):
                    j = jp * SPLIT + r
                    val = cur[pl.ds(jp * prev_len + r, sub_len, stride=SPLIT), :]
                    if g + 1 < ngroups:
                        nxt[j * sub_len:(j + 1) * sub_len, :] = val
                    put(kind, g, j * slot + front, val)
            cur, nxt = nxt, cur

    rows = lax.broadcasted_iota(jnp.int32, (blk, 2 * blk), 0)
    cols = lax.broadcasted_iota(jnp.int32, (blk, 2 * blk), 1)
    steps2 = rows + blk - cols
    rows1 = lax.broadcasted_iota(jnp.int32, (blk, blk), 0)
    cols1 = lax.broadcasted_iota(jnp.int32, (blk, blk), 1)
    steps1 = rows1 - cols1

    def scores(qt, kt, bias_of_head):
        out = []
        for hh in range(2):
            sel_q = first_q if hh == 0 else ~first_q
            qh = jnp.where(sel_q, qt, jnp.zeros_like(qt))
            s = lax.dot_general(qh, kt, (((1,), (1,)), ((), ())), preferred_element_type=F32)
            out.append(s + bias_of_head(hh))
        return out

    def weighted_values(ss, v_of_head):
        res, mx = [], []
        for hh in range(2):
            m = jnp.max(ss[hh], axis=-1, keepdims=True)
            p = jnp.exp2(ss[hh] - m).astype(BF16)
            res.append(_dot(p, v_of_head(hh)))
            mx.append(m)
        acc = jnp.where(first_q, res[0], res[1])
        den = pltpu.roll(jnp.where(first_q, res[1], res[0]), HEAD_DIM, 1)
        mm = jnp.where(first_q, mx[0], mx[1])
        return acc / den, mm + jnp.log2(den)

    for g, (window, dil) in enumerate(DILATION_GROUPS):
        n_back = window // dil
        sub_len = seq // dil
        nb = sub_len // blk
        slot = kv_slot_rows(g)
        prev_len = seq // dils[g - 1] if g else seq
        if nb == 1:
            dist_a = jnp.where((steps1 >= 0) & (steps1 <= n_back), steps1.astype(F32), MASK_STEPS)
            variants = (dist_a,)
            nk = blk
        else:
            valid = (steps2 >= 0) & (steps2 <= n_back)
            dist_a = jnp.where(valid, steps2.astype(F32), MASK_STEPS)
            dist_b = jnp.where(valid & (cols >= blk), steps2.astype(F32), MASK_STEPS)
            variants = (dist_a, dist_b)
            nk = 2 * blk
        for var, dist in enumerate(variants):
            for hh in range(2):
                bias[2 * var + hh, :, 0:nk] = (slopes_ref[2 * hp + hh] * (-LOG2E * dil)) * dist

        n_blocks = dil * nb
        pending = {}
        look = ATTN_LOOKAHEAD_SINGLE if nb == 1 else ATTN_LOOKAHEAD
        for step in range(n_blocks + look):
            if step < n_blocks:
                j, jb = divmod(step, nb)
                var = 1 if (nb > 1 and jb == 0) else 0
                pending[step] = scores(
                    qb[g, step * blk:(step + 1) * blk, :], kb[g, j * slot + jb * blk:j * slot + jb * blk + nk, :],
                    lambda hh, var=var, nk=nk: bias[2 * var + hh, :, 0:nk])
            u = step - look
            if u >= 0:
                j, jb = divmod(u, nb)
                k0 = j * slot + jb * blk
                o, l = weighted_values(pending.pop(u), lambda hh, g=g, k0=k0, nk=nk: vb[g, hh, k0:k0 + nk, :])
                if g == 0:
                    dst = pl.ds(u * blk, blk)
                else:
                    jp, r = divmod(j, SPLIT)
                    dst = pl.ds(jp * prev_len + r + SPLIT * jb * blk, blk, stride=SPLIT)
                og[g, dst, :] = o
                lg[g, dst, :] = l

    for res in (og, lg):
        for g in range(2, ngroups):
            for gg in range(g - 1, 0, -1):
                sub_len = seq // dils[gg]
                prev_len = seq // dils[gg - 1]
                sta[...] = res[g]
                for j in range(dils[gg - 1]):
                    for r in range(SPLIT):
                        src0 = (j * SPLIT + r) * sub_len
                        res[g, pl.ds(j * prev_len + r, sub_len, stride=SPLIT), :] = sta[src0:src0 + sub_len, :]

    for i in range(seq // blk):
        r0 = i * blk
        ls = [lg[g, pl.ds(r0, blk), :] for g in range(ngroups)]
        mmax = functools.reduce(jnp.maximum, ls)
        ws = [jnp.exp2(l - mmax) for l in ls]
        wsum = functools.reduce(lambda a, b: a + b, ws)
        acc = ws[0] * og[0, pl.ds(r0, blk), :]
        for g in range(1, ngroups):
            acc = acc + ws[g] * og[g, pl.ds(r0, blk), :]
        o_ref[pl.ds(r0, blk), :] = (acc / wsum).astype(o_ref.dtype)


def _attention(q, k, v, batch, seq):
    slopes = 2.0 ** (-8.0 * jnp.arange(1, N_HEADS + 1, dtype=F32) / N_HEADS)
    q3, k3, v3 = (t.reshape(batch, seq, ATTN_DIM) for t in (q, k, v))
    ngroups = len(DILATION_GROUPS)
    kv_rows = max(seq + d * (ATTN_BLOCK if seq // d > ATTN_BLOCK else 0) for _, d in DILATION_GROUPS)
    blk = pl.BlockSpec((None, seq, LANES), lambda b, h: (b, 0, h))
    out = pl.pallas_call(
        functools.partial(_attn_kernel, seq=seq),
        grid=(batch, ATTN_DIM // LANES),
        in_specs=[pl.BlockSpec(memory_space=pltpu.SMEM), blk, blk, blk],
        out_specs=blk,
        out_shape=jax.ShapeDtypeStruct((batch, seq, ATTN_DIM), BF16),
        scratch_shapes=[
            pltpu.VMEM((seq, LANES), F32),
            pltpu.VMEM((seq, LANES), F32),
            pltpu.VMEM((ngroups, seq, LANES), BF16),
            pltpu.VMEM((ngroups, kv_rows, LANES), BF16),
            pltpu.VMEM((ngroups, 2, kv_rows, LANES), BF16),
            pltpu.VMEM((4, ATTN_BLOCK, 2 * ATTN_BLOCK), F32),
            pltpu.VMEM((ngroups, seq, LANES), F32),
            pltpu.VMEM((ngroups, seq, LANES), F32),
        ],
        compiler_params=pltpu.CompilerParams(
            dimension_semantics=("parallel", "parallel"), vmem_limit_bytes=VMEM_LIMIT_BYTES),
        name="attention",
    )(slopes, q3, k3, v3)
    return out.reshape(batch * seq, ATTN_DIM)


def _token_out_kernel(x_ref, a_ref, halo_ref, o_ref, g_ref, cw_ref, cb_ref, cg_ref,
                      wc_ref, wa_ref, wo_ref, g2_ref, x1_ref, h2_ref, abuf, cbuf, *, tiles_per_seq):
    tm = TM_OUT
    first = (pl.program_id(0) % tiles_per_seq) == 0
    n_slabs = CONV_CH // LANES
    half = tm // 2
    u32 = jnp.uint32
    top16 = u32(0xFFFF0000)
    for c in range(n_slabs):
        lanes = slice(c * LANES, (c + 1) * LANES)
        halo = jnp.where(first, 0.0, halo_ref[:, lanes].astype(F32))
        bits = pltpu.bitcast(jnp.concatenate([halo, a_ref[:, lanes].astype(F32)], axis=0), u32)
        low = lax.shift_right_logical(bits[0:HALO + half, :], u32(16))
        abuf[c, pl.ds(0, HALO + half, stride=2), :] = low | (bits[half:HALO + tm, :] & top16)

    off = HALO - (CONV_WIDTH - 1)

    def conv_slab(c, carry):
        lanes = pl.ds(pl.multiple_of(c * LANES, LANES), LANES)
        acc_low = jnp.zeros((half, LANES), F32) + cb_ref[:, lanes]
        acc_high = acc_low
        for j0 in range(0, CONV_WIDTH, CONV_BF16_TAPS):
            group = None
            for j in range(j0, min(j0 + CONV_BF16_TAPS, CONV_WIDTH)):
                window = pltpu.bitcast(abuf[c, pl.ds(2 * (off + j), half, stride=2), :], BF16)
                term = cw_ref[pl.ds(j, 1), lanes].astype(BF16) * window
                group = term if group is None else group + term
            gbits = pltpu.bitcast(group, u32)
            acc_low = acc_low + pltpu.bitcast(lax.shift_left(gbits, u32(16)), F32)
            acc_high = acc_high + pltpu.bitcast(gbits & top16, F32)
        cbuf[pl.ds(0, half), lanes] = acc_low
        cbuf[pl.ds(half, half), lanes] = acc_high
        return carry

    lax.fori_loop(0, n_slabs, conv_slab, 0)

    cv = cbuf[...]
    ms = jnp.mean(cv * cv, axis=-1, keepdims=True)
    cn = cv * lax.rsqrt(ms + EPS) * cg_ref[...]
    ac = (cn * _sigmoid(cn)).astype(BF16)

    ya = _dot(ac, wc_ref[...])
    yb = _dot(o_ref[...], wa_ref[...])
    ga = g_ref[:, 0:D_MODEL].astype(F32)
    gb = g_ref[:, D_MODEL:2 * D_MODEL].astype(F32)
    mix = (ga * ya + gb * yb).astype(BF16)
    x1 = x_ref[...] + _dot(mix, wo_ref[...])
    x1_ref[...] = x1
    ms2 = jnp.mean(x1 * x1, axis=-1, keepdims=True)
    h2_ref[...] = (x1 * lax.rsqrt(ms2 + EPS) * g2_ref[...]).astype(BF16)


def _token_out(x2d, a, o, g, conv_w, conv_b, conv_norm_g, w_conv_out, w_attn_out, w_out, norm2_g, seq):
    t = x2d.shape[0]
    tm = TM_OUT
    tiles_per_seq = seq // tm
    halo_per_tile = tm // HALO
    cw = jnp.zeros((HALO, CONV_CH), F32).at[:CONV_WIDTH].set(conv_w.astype(F32))
    row = lambda i: (i, 0)
    return pl.pallas_call(
        functools.partial(_token_out_kernel, tiles_per_seq=tiles_per_seq),
        grid=(t // tm,),
        in_specs=[
            pl.BlockSpec((tm, D_MODEL), row),
            pl.BlockSpec((tm, CONV_CH), row),
            pl.BlockSpec((HALO, CONV_CH), lambda i: (jnp.maximum(i * halo_per_tile - 1, 0), 0)),
            pl.BlockSpec((tm, ATTN_DIM), row),
            pl.BlockSpec((tm, 2 * D_MODEL), row),
            _resident((HALO, CONV_CH)),
            _resident((1, CONV_CH)),
            _resident((1, CONV_CH)),
            _resident((CONV_CH, D_MODEL)),
            _resident((ATTN_DIM, D_MODEL)),
            _resident((D_MODEL, D_MODEL)),
            _resident((1, D_MODEL)),
        ],
        out_specs=[pl.BlockSpec((tm, D_MODEL), row), pl.BlockSpec((tm, D_MODEL), row)],
        out_shape=[jax.ShapeDtypeStruct((t, D_MODEL), F32), jax.ShapeDtypeStruct((t, D_MODEL), BF16)],
        scratch_shapes=[pltpu.VMEM((CONV_CH // LANES, 2 * (HALO + tm // 2), LANES), jnp.uint32),
                        pltpu.VMEM((tm, CONV_CH), F32)],
        compiler_params=pltpu.CompilerParams(
            dimension_semantics=("parallel",), vmem_limit_bytes=VMEM_LIMIT_BYTES),
        name="token_out",
    )(x2d, a, a, o, g, cw, conv_b.reshape(1, CONV_CH), conv_norm_g.reshape(1, CONV_CH),
      w_conv_out.astype(BF16), w_attn_out.astype(BF16), w_out.astype(BF16), norm2_g.reshape(1, D_MODEL))


def _ffn_kernel(x1_ref, h2_ref, wup_ref, fw_ref, fb_ref, wdn_ref, x2_ref, ubuf, carry, act, *, tiles_per_seq):
    tm = TM_FFN
    first = (pl.program_id(0) % tiles_per_seq) == 0

    @pl.when(first)
    def _():
        carry[...] = jnp.zeros_like(carry)

    h2 = h2_ref[...]
    top = SUBLANES
    taps = FFN_CONV_WIDTH

    def conv(col0):
        u = _dot(h2, wup_ref[:, col0:col0 + FFN_CHUNK])
        ubuf[0:top, :] = carry[:, col0:col0 + FFN_CHUNK]
        ubuf[top:top + tm, :] = u
        carry[:, col0:col0 + FFN_CHUNK] = u[tm - top:tm, :]
        out = fb_ref[:, col0:col0 + FFN_CHUNK] + fw_ref[taps - 1:taps, col0:col0 + FFN_CHUNK] * u
        for j in range(taps - 1):
            shift = taps - 1 - j
            out = out + fw_ref[j:j + 1, col0:col0 + FFN_CHUNK] * ubuf[top - shift:top - shift + tm, :]
        return out

    for c in range(0, D_FF, FFN_CHUNK):
        uv = conv(c)
        ug = conv(D_FF + c)
        act[:, c:c + FFN_CHUNK] = (ug * _sigmoid(ug) * uv).astype(BF16)

    x2_ref[...] = x1_ref[...] + _dot(act[...], wdn_ref[...])


def _ffn(x1, h2, w_up, ffn_conv_w, ffn_conv_b, w_down, seq):
    t = x1.shape[0]
    tm = TM_FFN
    fw = jnp.zeros((SUBLANES, 2 * D_FF), F32).at[:FFN_CONV_WIDTH].set(ffn_conv_w.astype(F32))
    row = lambda i: (i, 0)
    return pl.pallas_call(
        functools.partial(_ffn_kernel, tiles_per_seq=seq // tm),
        grid=(t // tm,),
        in_specs=[
            pl.BlockSpec((tm, D_MODEL), row),
            pl.BlockSpec((tm, D_MODEL), row),
            _resident((D_MODEL, 2 * D_FF)),
            _resident((SUBLANES, 2 * D_FF)),
            _resident((1, 2 * D_FF)),
            _resident((D_FF, D_MODEL)),
        ],
        out_specs=pl.BlockSpec((tm, D_MODEL), row),
        out_shape=jax.ShapeDtypeStruct((t, D_MODEL), F32),
        scratch_shapes=[
            pltpu.VMEM((SUBLANES + tm, FFN_CHUNK), F32),
            pltpu.VMEM((SUBLANES, 2 * D_FF), F32),
            pltpu.VMEM((tm, D_FF), BF16),
        ],
        compiler_params=pltpu.CompilerParams(
            dimension_semantics=("arbitrary",), vmem_limit_bytes=VMEM_LIMIT_BYTES),
        name="ffn",
    )(x1, h2, w_up.astype(BF16), fw, ffn_conv_b.reshape(1, 2 * D_FF), w_down.astype(BF16))


def kernel(x, norm1_g, w_in, gate_b, conv_w, conv_b, conv_norm_g, w_conv_out, q_norm_g, k_norm_g,
           w_attn_out, w_out, norm2_g, w_up, ffn_conv_w, ffn_conv_b, w_down):
    batch, seq, d = x.shape
    assert d == D_MODEL and seq % TM_IN == 0 and seq % TM_OUT == 0 and seq % TM_FFN == 0
    assert all(seq % (dil * ATTN_BLOCK) == 0 for _, dil in DILATION_GROUPS)
    depth = norm1_g.shape[0]
    x2d = x.reshape(batch * seq, d)
    for l in range(depth):
        a, q, k, v, g = _in_proj(x2d, norm1_g[l], w_in[l], gate_b[l], q_norm_g[l], k_norm_g[l])
        o = _attention(q, k, v, batch, seq)
        x1, h2 = _token_out(x2d, a, o, g, conv_w[l], conv_b[l], conv_norm_g[l], w_conv_out[l],
                            w_attn_out[l], w_out[l], norm2_g[l], seq)
        x2d = _ffn(x1, h2, w_up[l], ffn_conv_w[l], ffn_conv_b[l], w_down[l], seq)
    return x2d.reshape(batch, seq, d)
```
